```python
import jax, jax.numpy as jnp
from jax import lax
import numpy as np

D_MODEL = 2048
BATCH = 4
SEQ = 8192
DEPTH = 1

N_META = 16
EPS = 1e-6
CONV_W = 4
LRU_WIDTH = D_MODEL // 2
LRU_HEADS = 8
LRU_HEAD_DIM = LRU_WIDTH // LRU_HEADS
LRU_C = 8.0
SSD_INNER = D_MODEL
SSD_HEAD_DIM = 64
SSD_HEADS = SSD_INNER // SSD_HEAD_DIM
SSD_GROUPS = 4
SSD_HEADS_PER_GROUP = SSD_HEADS // SSD_GROUPS
SSD_STATE = 128
SSD_CHUNK = 128
SSD_CONV_DIM = SSD_INNER + 2 * SSD_GROUPS * SSD_STATE
N_EXPERTS = 32
TOP_K = 4
D_FF = D_MODEL
SWIGLU_LIMIT = 7.0
SWIGLU_ALPHA = 1.702
MOE_BLOCK = 512
IN_SPLITS = (LRU_WIDTH, LRU_WIDTH, SSD_INNER, SSD_CONV_DIM, SSD_HEADS, D_MODEL, D_MODEL)
IN_WIDTH = sum(IN_SPLITS)

kernel_name = 'hawk_mamba2_gated_moe_block'


def _split_points(sizes):
    pts, acc = [], 0
    for s in sizes[:-1]:
        acc += s
        pts.append(acc)
    return pts


def _rms(xf):
    return xf * lax.rsqrt(jnp.mean(xf * xf, axis=-1, keepdims=True) + EPS)


def rmsnorm(x, g):
    return (_rms(x.astype(jnp.float32)) * g.astype(jnp.float32)).astype(x.dtype)


def causal_dwconv(x, w, b):
    k, c = w.shape
    y = lax.conv_general_dilated(x, w.astype(x.dtype)[:, None, :], window_strides=(1,),
                                 padding=[(k - 1, 0)], dimension_numbers=('NWC', 'WIO', 'NWC'),
                                 feature_group_count=c)
    return y + b.astype(x.dtype)


def rg_lru(x, w_a, b_a, w_i, b_i, lam):
    bsz, t, w = x.shape
    xh = x.reshape(bsz, t, LRU_HEADS, LRU_HEAD_DIM)
    r = jax.nn.sigmoid(jnp.einsum('bthi,hij->bthj', xh, w_a) + b_a).reshape(bsz, t, w)
    i = jax.nn.sigmoid(jnp.einsum('bthi,hij->bthj', xh, w_i) + b_i).reshape(bsz, t, w)
    log_a = LRU_C * r.astype(jnp.float32) * jax.nn.log_sigmoid(lam.astype(jnp.float32))
    a = jnp.exp(log_a)
    mult = jnp.sqrt(-jnp.expm1(2.0 * log_a))
    mult = jnp.where(jnp.arange(t)[None, :, None] == 0, 1.0, mult)
    u = mult * (i * x).astype(jnp.float32)

    def combine(lhs, rhs):
        a1, b1 = lhs
        a2, b2 = rhs
        return a1 * a2, a2 * b1 + b2

    _, hs = lax.associative_scan(combine, (a, u), axis=1)
    return hs.astype(x.dtype)


def ssd_chunked(x, dt, a, bm, cm):
    bsz, t = x.shape[:2]
    nc, ln = t // SSD_CHUNK, SSD_CHUNK
    g, r, p, n = SSD_GROUPS, SSD_HEADS_PER_GROUP, SSD_HEAD_DIM, SSD_STATE
    xr = x.astype(jnp.float32).reshape(bsz, nc, ln, g, r, p)
    dtr = dt.reshape(bsz, nc, ln, g, r)
    br = bm.astype(jnp.float32).reshape(bsz, nc, ln, g, n)
    cr = cm.astype(jnp.float32).reshape(bsz, nc, ln, g, n)
    cs = jnp.cumsum(dtr * a.reshape(g, r), axis=2)
    xdt = xr * dtr[..., None]
    mask = jnp.tril(jnp.ones((ln, ln), dtype=bool))[None, None, :, :, None, None]
    seg = cs[:, :, :, None] - cs[:, :, None]
    lmat = jnp.exp(jnp.where(mask, seg, -jnp.inf))
    cb = jnp.einsum('bclgn,bcsgn->bclsg', cr, br)
    y_diag = jnp.einsum('bclsgr,bcsgrp->bclgrp', cb[..., None] * lmat, xdt)
    decay_states = jnp.exp(cs[:, :, -1:] - cs)
    states = jnp.einsum('bclgn,bclgrp->bcgrpn', br, xdt * decay_states[..., None])
    chunk_decay = jnp.exp(cs[:, :, -1])

    def step(h, inp):
        s, d = inp
        return h * d[..., None, None] + s, h

    h0 = jnp.zeros((bsz, g, r, p, n), jnp.float32)
    _, prev = lax.scan(step, h0, (jnp.moveaxis(states, 1, 0), jnp.moveaxis(chunk_decay, 1, 0)))
    prev = jnp.moveaxis(prev, 0, 1)
    y_off = jnp.einsum('bclgn,bcgrpn->bclgrp', cr, prev) * jnp.exp(cs)[..., None]
    return (y_diag + y_off).reshape(bsz, t, SSD_HEADS, p)


def hybrid_mixer(hn, w_in, gate_bias, lru_conv_w, lru_conv_b, lru_wa, lru_ba, lru_wi, lru_bi,
                 lru_lambda, ssd_conv_w, ssd_conv_b, ssd_dt_bias, ssd_A_log, ssd_D, ssd_norm_g,
                 w_proj_lru, w_proj_ssd, w_out):
    bsz, t, _ = hn.shape
    proj = hn @ w_in
    xa, ya, z, xbc, dt_raw, ga, gb = jnp.split(proj, _split_points(IN_SPLITS), axis=-1)
    xa = causal_dwconv(xa, lru_conv_w, lru_conv_b)
    ha = rg_lru(xa, lru_wa, lru_ba, lru_wi, lru_bi, lru_lambda)
    out_a = (jax.nn.gelu(ya) * ha) @ w_proj_lru
    xbc = jax.nn.silu(causal_dwconv(xbc, ssd_conv_w, ssd_conv_b))
    xs, bs, cs = jnp.split(xbc, [SSD_INNER, SSD_INNER + SSD_GROUPS * SSD_STATE], axis=-1)
    dt = jax.nn.softplus(dt_raw.astype(jnp.float32) + ssd_dt_bias.astype(jnp.float32))
    a = -jnp.exp(ssd_A_log.astype(jnp.float32))
    pad = (-N_META) % SSD_CHUNK

    def padt(u):
        return jnp.pad(u, ((0, 0), (pad, 0)) + ((0, 0),) * (u.ndim - 2))

    xh = xs.reshape(bsz, t, SSD_HEADS, SSD_HEAD_DIM)
    y = ssd_chunked(padt(xh), padt(dt), a,
                    padt(bs.reshape(bsz, t, SSD_GROUPS, SSD_STATE)),
                    padt(cs.reshape(bsz, t, SSD_GROUPS, SSD_STATE)))[:, pad:]
    y = y + xh.astype(jnp.float32) * ssd_D.astype(jnp.float32)[:, None]
    y = y.reshape(bsz, t, SSD_INNER) * jax.nn.silu(z.astype(jnp.float32))
    y = _rms(y.reshape(bsz, t, SSD_GROUPS, SSD_INNER // SSD_GROUPS)).reshape(bsz, t, SSD_INNER)
    y = (y * ssd_norm_g.astype(jnp.float32)).astype(hn.dtype)
    out_b = y @ w_proj_ssd
    g_a = jax.nn.sigmoid(ga + gate_bias[0])
    g_b = jax.nn.sigmoid(gb + gate_bias[1])
    return (g_a * out_a + g_b * out_b) @ w_out


def moe_ffn(hn, w_router, b_router, w_gate_up, b_gate_up, w_down, b_down):
    bsz, t, d = hn.shape
    m = bsz * t
    xt = hn.reshape(m, d)
    logits = xt.astype(jnp.float32) @ w_router.astype(jnp.float32) + b_router.astype(jnp.float32)
    top_v, top_e = lax.top_k(logits, TOP_K)
    top_w = jax.nn.softmax(top_v, axis=-1)
    n_assign = m * TOP_K
    flat_e = top_e.reshape(-1).astype(jnp.int32)
    flat_w = top_w.reshape(-1)
    flat_tok = jnp.repeat(jnp.arange(m, dtype=jnp.int32), TOP_K)
    order = jnp.argsort(flat_e, stable=True)
    sorted_e = flat_e[order]
    counts = jnp.bincount(flat_e, length=N_EXPERTS).astype(jnp.int32)
    starts = jnp.cumsum(counts) - counts
    padded_counts = (counts + MOE_BLOCK - 1) // MOE_BLOCK * MOE_BLOCK
    padded_end = jnp.cumsum(padded_counts)
    padded_start = padded_end - padded_counts
    dest = padded_start[sorted_e] + jnp.arange(n_assign, dtype=jnp.int32) - starts[sorted_e]
    n_blocks = -(-n_assign // MOE_BLOCK) + N_EXPERTS
    n_rows = n_blocks * MOE_BLOCK
    row_tok = jnp.zeros((n_rows,), jnp.int32).at[dest].set(flat_tok[order])
    row_w = jnp.zeros((n_rows,), jnp.float32).at[dest].set(flat_w[order])
    block_e = jnp.minimum(jnp.searchsorted(padded_end, jnp.arange(n_blocks, dtype=jnp.int32) * MOE_BLOCK,
                                           side='right'), N_EXPERTS - 1)

    def expert_block(args):
        e, tok, wt = args
        xb = xt[tok]
        gu = xb @ w_gate_up[e] + b_gate_up[e]
        gate, up = gu[:, :D_FF], gu[:, D_FF:]
        gate = jnp.minimum(gate, SWIGLU_LIMIT)
        up = jnp.clip(up, -SWIGLU_LIMIT, SWIGLU_LIMIT)
        act = (up + 1.0) * gate * jax.nn.sigmoid(SWIGLU_ALPHA * gate)
        out = act @ w_down[e] + b_down[e]
        return out * wt[:, None].astype(out.dtype)

    rows = lax.map(expert_block, (block_e, row_tok.reshape(n_blocks, MOE_BLOCK),
                                  row_w.reshape(n_blocks, MOE_BLOCK)))
    y = jax.ops.segment_sum(rows.reshape(n_rows, d), row_tok, num_segments=m)
    return y.reshape(bsz, t, d)


def setup_inputs(seed: int = 0) -> dict:
    key = jax.random.key(seed)
    ks = jax.random.split(key, 32)
    f32 = jnp.float32

    def nrm(k, shape, scale):
        return jax.random.normal(k, shape, f32) * scale

    L = DEPTH
    a8 = jax.random.uniform(ks[11], (L, LRU_WIDTH), f32, 0.9, 0.999)
    a0 = a8 ** (1.0 / LRU_C)
    lru_lambda = jnp.log(a0) - jnp.log1p(-a0)
    dt0 = jnp.exp(jax.random.uniform(ks[14], (L, SSD_HEADS), f32, np.log(1e-3), np.log(1e-1)))
    ssd_dt_bias = dt0 + jnp.log(-jnp.expm1(-dt0))
    ssd_A_log = jnp.log(jax.random.uniform(ks[15], (L, SSD_HEADS), f32, 1.0, 16.0))
    return {
        'x': nrm(ks[0], (BATCH, SEQ, D_MODEL), 1.0),
        'meta_tokens': nrm(ks[1], (N_META, D_MODEL), 1.0),
        'norm_mix_g': 1.0 + nrm(ks[2], (L, D_MODEL), 0.02),
        'w_in': nrm(ks[3], (L, D_MODEL, IN_WIDTH), D_MODEL ** -0.5),
        'gate_bias': nrm(ks[4], (L, 2, D_MODEL), 0.1),
        'lru_conv_w': nrm(ks[5], (L, CONV_W, LRU_WIDTH), CONV_W ** -0.5),
        'lru_conv_b': nrm(ks[6], (L, LRU_WIDTH), 0.02),
        'lru_wa': nrm(ks[7], (L, LRU_HEADS, LRU_HEAD_DIM, LRU_HEAD_DIM), LRU_HEAD_DIM ** -0.5),
        'lru_ba': nrm(ks[8], (L, LRU_HEADS, LRU_HEAD_DIM), 0.02),
        'lru_wi': nrm(ks[9], (L, LRU_HEADS, LRU_HEAD_DIM, LRU_HEAD_DIM), LRU_HEAD_DIM ** -0.5),
        'lru_bi': nrm(ks[10], (L, LRU_HEADS, LRU_HEAD_DIM), 0.02),
        'lru_lambda': lru_lambda,
        'ssd_conv_w': nrm(ks[12], (L, CONV_W, SSD_CONV_DIM), CONV_W ** -0.5),
        'ssd_conv_b': nrm(ks[13], (L, SSD_CONV_DIM), 0.02),
        'ssd_dt_bias': ssd_dt_bias,
        'ssd_A_log': ssd_A_log,
        'ssd_D': 1.0 + nrm(ks[16], (L, SSD_HEADS), 0.1),
        'ssd_norm_g': 1.0 + nrm(ks[17], (L, SSD_INNER), 0.02),
        'w_proj_lru': nrm(ks[18], (L, LRU_WIDTH, D_MODEL), LRU_WIDTH ** -0.5),
        'w_proj_ssd': nrm(ks[19], (L, SSD_INNER, D_MODEL), SSD_INNER ** -0.5),
        'w_out': nrm(ks[20], (L, D_MODEL, D_MODEL), D_MODEL ** -0.5),
        'norm_moe_g': 1.0 + nrm(ks[21], (L, D_MODEL), 0.02),
        'w_router': nrm(ks[22], (L, D_MODEL, N_EXPERTS), D_MODEL ** -0.5),
        'b_router': nrm(ks[23], (L, N_EXPERTS), 0.01),
        'w_gate_up': nrm(ks[24], (L, N_EXPERTS, D_MODEL, 2 * D_FF), D_MODEL ** -0.5),
        'b_gate_up': nrm(ks[25], (L, N_EXPERTS, 2 * D_FF), 0.02),
        'w_down': nrm(ks[26], (L, N_EXPERTS, D_FF, D_MODEL), D_FF ** -0.5),
        'b_down': nrm(ks[27], (L, N_EXPERTS, D_MODEL), 0.02),
        'norm_final_g': 1.0 + nrm(ks[28], (D_MODEL,), 0.02),
    }


def reference(x, meta_tokens, norm_mix_g, w_in, gate_bias, lru_conv_w, lru_conv_b, lru_wa, lru_ba,
              lru_wi, lru_bi, lru_lambda, ssd_conv_w, ssd_conv_b, ssd_dt_bias, ssd_A_log, ssd_D,
              ssd_norm_g, w_proj_lru, w_proj_ssd, w_out, norm_moe_g, w_router, b_router, w_gate_up,
              b_gate_up, w_down, b_down, norm_final_g):
    bsz = x.shape[0]
    meta = jnp.broadcast_to(meta_tokens.astype(x.dtype)[None], (bsz, N_META, D_MODEL))
    h = jnp.concatenate([meta, x], axis=1)
    for l in range(DEPTH):
        h = h + hybrid_mixer(rmsnorm(h, norm_mix_g[l]), w_in[l], gate_bias[l], lru_conv_w[l],
                             lru_conv_b[l], lru_wa[l], lru_ba[l], lru_wi[l], lru_bi[l], lru_lambda[l],
                             ssd_conv_w[l], ssd_conv_b[l], ssd_dt_bias[l], ssd_A_log[l], ssd_D[l],
                             ssd_norm_g[l], w_proj_lru[l], w_proj_ssd[l], w_out[l])
        h = h + moe_ffn(rmsnorm(h, norm_moe_g[l]), w_router[l], b_router[l], w_gate_up[l],
                        b_gate_up[l], w_down[l], b_down[l])
    return rmsnorm(h, norm_final_g)[:, N_META:]
```

```python
import functools

import jax
import jax.numpy as jnp
from jax import lax
from jax.experimental import pallas as pl
from jax.experimental.pallas import tpu as pltpu

F32 = jnp.float32
BF16 = jnp.bfloat16

D_MODEL = 2048
N_META = 16
EPS = 1e-6
CONV_W = 4
LRU_WIDTH = 1024
LRU_HEADS = 8
LRU_HEAD_DIM = 128
LRU_C = 8.0
SSD_INNER = 2048
SSD_HEAD_DIM = 64
SSD_HEADS = 32
SSD_GROUPS = 4
SSD_STATE = 128
SSD_CHUNK = 128
SSD_PAIRS = SSD_HEADS // 2
N_EXPERTS = 32
TOP_K = 4
D_FF = 2048
SWIGLU_LIMIT = 7.0
SWIGLU_ALPHA = 1.702

LANES = 128
SUBLANES = 8
ROW_CHUNKS = D_MODEL // LANES
VMEM_LIMIT = 56 * 1024 * 1024

COL_XA, COL_YA, COL_Z, COL_XS, COL_BC, COL_GA, COL_GB = 0, 1024, 2048, 4096, 6144, 7168, 9216
PROJ_W = 11264


def _cparams(sem):
    return pltpu.CompilerParams(dimension_semantics=sem, vmem_limit_bytes=VMEM_LIMIT)


def _sigmoid(v):
    return 1.0 / (1.0 + jnp.exp(-v))


def _silu(v):
    return v * _sigmoid(v)


def _softplus(v):
    return jnp.maximum(v, 0.0) + jnp.log1p(jnp.exp(-jnp.abs(v)))


def _gelu_tanh(v):
    return 0.5 * v * (1.0 + jnp.tanh(0.7978845608028654 * (v + 0.044715 * v * v * v)))


def _pick(n, candidates):
    for c in candidates:
        if n % c == 0:
            return c
    raise ValueError(f"no tile for {n}")


def _inproj_kernel(x_ref, g_ref, w_ref, wdt_ref, o_ref, odt_ref, xn_ref):
    @pl.when(pl.program_id(1) == 0)
    def _():
        x = x_ref[...]
        ms = jnp.mean(x * x, axis=-1, keepdims=True)
        xn = (x * lax.rsqrt(ms + EPS) * g_ref[...]).astype(BF16)
        xn_ref[...] = xn
        odt_ref[...] = jnp.dot(xn, wdt_ref[...], preferred_element_type=F32)

    o_ref[...] = jnp.dot(xn_ref[...], w_ref[...], preferred_element_type=F32)


def _inproj(xr, g, w_main, w_dt):
    rows = xr.shape[0]
    tm = _pick(rows, (1024, 512, 256, 128))
    tn = 1024
    return pl.pallas_call(
        _inproj_kernel,
        grid=(rows // tm, PROJ_W // tn),
        in_specs=[
            pl.BlockSpec((tm, D_MODEL), lambda i, j: (i, 0)),
            pl.BlockSpec((1, D_MODEL), lambda i, j: (0, 0)),
            pl.BlockSpec((D_MODEL, tn), lambda i, j: (0, j)),
            pl.BlockSpec((D_MODEL, LANES), lambda i, j: (0, 0)),
        ],
        out_specs=[
            pl.BlockSpec((tm, tn), lambda i, j: (i, j)),
            pl.BlockSpec((tm, LANES), lambda i, j: (i, 0)),
        ],
        out_shape=[
            jax.ShapeDtypeStruct((rows, PROJ_W), F32),
            jax.ShapeDtypeStruct((rows, LANES), F32),
        ],
        scratch_shapes=[pltpu.VMEM((tm, D_MODEL), BF16)],
        compiler_params=_cparams(("parallel", "arbitrary")),
        name="inproj",
    )(xr, g, w_main, w_dt)


def _causal_conv(x, tail, cw, cb):
    n = x.shape[0]
    ext = jnp.concatenate([tail, x], axis=0)
    y = cb + cw[CONV_W - 1:CONV_W] * x
    for k in range(1, CONV_W):
        y = y + cw[CONV_W - 1 - k:CONV_W - k] * pltpu.roll(ext, k, axis=0)[SUBLANES:SUBLANES + n]
    return y


def _lru_kernel(xa_ref, ya_ref, cw_ref, cb_ref, wai_ref, ba_ref, bi_ref, lam_ref, tail0_ref, h0_ref,
                o_ref, tail_out_ref, h_out_ref, tail_sc, h_sc, *, tb, seq_start):
    t = pl.program_id(1)

    @pl.when(t == 0)
    def _():
        tail_sc[...] = tail0_ref[...]
        h_sc[...] = h0_ref[...]

    xa = xa_ref[...]
    xc = _causal_conv(xa, tail_sc[...], cw_ref[...], cb_ref[...])
    tail_sc[...] = xa[tb - SUBLANES:tb]

    xcb = xc.astype(BF16)
    r_parts, i_parts = [], []
    for h in range(LRU_HEADS):
        g = jnp.dot(xcb[:, h * LRU_HEAD_DIM:(h + 1) * LRU_HEAD_DIM], wai_ref[h],
                    preferred_element_type=F32)
        r_parts.append(g[:, :LRU_HEAD_DIM])
        i_parts.append(g[:, LRU_HEAD_DIM:])
    r = _sigmoid(jnp.concatenate(r_parts, axis=1) + ba_ref[...])
    ig = _sigmoid(jnp.concatenate(i_parts, axis=1) + bi_ref[...])

    log_a = LRU_C * r * (-_softplus(-lam_ref[...]))
    a = jnp.exp(log_a)
    mult = jnp.sqrt(-jnp.tanh(log_a) * (a * a + 1.0))
    row = lax.broadcasted_iota(jnp.int32, (tb, LRU_WIDTH), 0)
    if seq_start:
        mult = jnp.where((row == 0) & (t == 0), 1.0, mult)
    u = mult * (ig * xc)

    s = 1
    while s < tb:
        a_sh = pltpu.roll(a, s, axis=0)
        u_sh = pltpu.roll(u, s, axis=0)
        valid = row >= s
        u = jnp.where(valid, a * u_sh, 0.0) + u
        a = jnp.where(valid, a * a_sh, a)
        s *= 2
    h = a * h_sc[0:1, :] + u
    h_last = jnp.broadcast_to(h[tb - 1:tb, :], (SUBLANES, LRU_WIDTH))
    h_sc[...] = h_last
    o_ref[...] = (_gelu_tanh(ya_ref[...]) * h).astype(BF16)
    tail_out_ref[...] = xa[tb - SUBLANES:tb]
    h_out_ref[...] = h_last


def _lru(proj, nb, t_len, row0_blocks, tb, cw, cb, wai, ba, bi, lam, tail0, h0, seq_start):
    nt = t_len // tb
    kern = functools.partial(_lru_kernel, tb=tb, seq_start=seq_start)
    per_b = (proj.shape[0] // nb) // tb
    return pl.pallas_call(
        kern,
        grid=(nb, nt),
        in_specs=[
            pl.BlockSpec((tb, LRU_WIDTH), lambda b, t: (b * per_b + row0_blocks + t, COL_XA // LRU_WIDTH)),
            pl.BlockSpec((tb, LRU_WIDTH), lambda b, t: (b * per_b + row0_blocks + t, COL_YA // LRU_WIDTH)),
            pl.BlockSpec((CONV_W, LRU_WIDTH), lambda b, t: (0, 0)),
            pl.BlockSpec((1, LRU_WIDTH), lambda b, t: (0, 0)),
            pl.BlockSpec((LRU_HEADS, LRU_HEAD_DIM, 2 * LRU_HEAD_DIM), lambda b, t: (0, 0, 0)),
            pl.BlockSpec((1, LRU_WIDTH), lambda b, t: (0, 0)),
            pl.BlockSpec((1, LRU_WIDTH), lambda b, t: (0, 0)),
            pl.BlockSpec((1, LRU_WIDTH), lambda b, t: (0, 0)),
            pl.BlockSpec((SUBLANES, LRU_WIDTH), lambda b, t: (0, 0)),
            pl.BlockSpec((SUBLANES, LRU_WIDTH), lambda b, t: (0, 0)),
        ],
        out_specs=[
            pl.BlockSpec((tb, LRU_WIDTH), lambda b, t: (b * nt + t, 0)),
            pl.BlockSpec((None, SUBLANES, LRU_WIDTH), lambda b, t: (b, 0, 0)),
            pl.BlockSpec((None, SUBLANES, LRU_WIDTH), lambda b, t: (b, 0, 0)),
        ],
        out_shape=[
            jax.ShapeDtypeStruct((nb * t_len, LRU_WIDTH), BF16),
            jax.ShapeDtypeStruct((nb, SUBLANES, LRU_WIDTH), F32),
            jax.ShapeDtypeStruct((nb, SUBLANES, LRU_WIDTH), F32),
        ],
        scratch_shapes=[pltpu.VMEM((SUBLANES, LRU_WIDTH), F32), pltpu.VMEM((SUBLANES, LRU_WIDTH), F32)],
        compiler_params=_cparams(("arbitrary", "arbitrary")),
        name="lru",
    )(proj, proj, cw, cb, wai, ba, bi, lam, tail0, h0)


def _ssd_kernel(xs_ref, bc_ref, z_ref, dt_ref, cwx_ref, cbx_ref, cwb_ref, cbb_ref, dtb_ref, alog_ref,
                dexp_ref, ng_ref, tailx0_ref, tailb0_ref, s0_ref,
                y_ref, tailx_out_ref, tailb_out_ref, s_out_ref,
                tailx_sc, tailb_sc, s_sc, *, first_valid):
    c = pl.program_id(1)
    ln = SSD_CHUNK

    @pl.when(c == 0)
    def _():
        tailx_sc[...] = tailx0_ref[...]
        tailb_sc[...] = tailb0_ref[...]
        s_sc[...] = s0_ref[...]

    xr = xs_ref[...]
    br = bc_ref[...]
    xs = _silu(_causal_conv(xr, tailx_sc[...], cwx_ref[...], cbx_ref[...]))
    bc = _silu(_causal_conv(br, tailb_sc[...], cwb_ref[...], cbb_ref[...]))
    tailx_sc[...] = xr[ln - SUBLANES:ln]
    tailb_sc[...] = br[ln - SUBLANES:ln]
    dt = _softplus(dt_ref[...] + dtb_ref[...])
    if first_valid:
        keep = lax.broadcasted_iota(jnp.int32, (ln, 1), 0) >= first_valid
        xs = jnp.where(keep, xs, 0.0)
        bc = jnp.where(keep, bc, 0.0)
        dt = jnp.where(keep, dt, 0.0)

    a_neg = -jnp.exp(alog_ref[...])
    da = dt * a_neg
    rr = lax.broadcasted_iota(jnp.int32, (ln, ln), 0)
    cc = lax.broadcasted_iota(jnp.int32, (ln, ln), 1)
    causal = rr >= cc
    cs = jnp.dot(causal.astype(F32), da, preferred_element_type=F32,
                 precision=lax.Precision.HIGHEST)
    cs_last = cs[ln - 1:ln, :]
    ecs = jnp.exp(cs)
    wdec = jnp.exp(cs_last - cs) * dt
    cs_t = cs.T
    dt_t = dt.T
    wdec_t = wdec.T
    dec_col = jnp.exp(cs_t[:, ln - 1:ln])

    lane = lax.broadcasted_iota(jnp.int32, (1, LANES), 1)
    left = lane < SSD_HEAD_DIM
    xsb = xs.astype(BF16)
    y_parts = []
    for g in range(SSD_GROUPS):
        bg = bc[:, g * SSD_STATE:(g + 1) * SSD_STATE]
        cg = bc[:, (SSD_GROUPS + g) * SSD_STATE:(SSD_GROUPS + g + 1) * SSD_STATE]
        cb_mat = lax.dot_general(cg.astype(BF16), bg.astype(BF16), (((1,), (1,)), ((), ())),
                                 preferred_element_type=F32)
        bg_t = bg.T
        for jj in range(SSD_PAIRS // SSD_GROUPS):
            j = g * (SSD_PAIRS // SSD_GROUPS) + jj
            xp = xsb[:, j * LANES:(j + 1) * LANES]
            sp = s_sc[j]
            rhs = jnp.concatenate([xp, sp.astype(BF16)], axis=0)
            y_h, s_h, d_h = [], [], []
            for h in (2 * j, 2 * j + 1):
                seg = cs[:, h:h + 1] - cs_t[h:h + 1, :]
                lmat = jnp.exp(jnp.where(causal, seg, -jnp.inf)) * dt_t[h:h + 1, :]
                m_h = (cb_mat * lmat).astype(BF16)
                c_off = (cg * ecs[:, h:h + 1]).astype(BF16)
                y_h.append(jnp.dot(jnp.concatenate([m_h, c_off], axis=1), rhs,
                                   preferred_element_type=F32))
                bw_t = (bg_t * wdec_t[h:h + 1, :]).astype(BF16)
                s_h.append(jnp.dot(bw_t, xp, preferred_element_type=F32))
                d_h.append(dec_col[h:h + 1, :])
            y_parts.append(jnp.where(left, y_h[0], y_h[1]))
            s_sc[j] = sp * jnp.where(left, d_h[0], d_h[1]) + jnp.where(left, s_h[0], s_h[1])

    y = jnp.concatenate(y_parts, axis=1) + xs * dexp_ref[...]
    y = y * _silu(z_ref[...])
    gw = SSD_INNER // SSD_GROUPS
    outs = []
    for g in range(SSD_GROUPS):
        yg = y[:, g * gw:(g + 1) * gw]
        outs.append(yg * lax.rsqrt(jnp.mean(yg * yg, axis=-1, keepdims=True) + EPS))
    y_ref[...] = (jnp.concatenate(outs, axis=1) * ng_ref[...]).astype(BF16)
    tailx_out_ref[...] = xr[ln - SUBLANES:ln]
    tailb_out_ref[...] = br[ln - SUBLANES:ln]
    s_out_ref[...] = s_sc[...]


def _ssd(proj, dtraw, nb, t_len, row0_blocks, params, tailx0, tailb0, s0, first_valid):
    ln = SSD_CHUNK
    nc = t_len // ln
    per_b = (proj.shape[0] // nb) // ln
    cwx, cbx, cwb, cbb, dtb, alog, dexp, ng = params
    kern = functools.partial(_ssd_kernel, first_valid=first_valid)
    const2 = lambda b, c: (0, 0)
    rowi = lambda b, c: b * per_b + row0_blocks + c
    return pl.pallas_call(
        kern,
        grid=(nb, nc),
        in_specs=[
            pl.BlockSpec((ln, SSD_INNER), lambda b, c: (rowi(b, c), COL_XS // SSD_INNER)),
            pl.BlockSpec((ln, 1024), lambda b, c: (rowi(b, c), COL_BC // 1024)),
            pl.BlockSpec((ln, SSD_INNER), lambda b, c: (rowi(b, c), COL_Z // SSD_INNER)),
            pl.BlockSpec((ln, LANES), lambda b, c: (rowi(b, c), 0)),
            pl.BlockSpec((CONV_W, SSD_INNER), const2),
            pl.BlockSpec((1, SSD_INNER), const2),
            pl.BlockSpec((CONV_W, 1024), const2),
            pl.BlockSpec((1, 1024), const2),
            pl.BlockSpec((1, LANES), const2),
            pl.BlockSpec((1, LANES), const2),
            pl.BlockSpec((1, SSD_INNER), const2),
            pl.BlockSpec((1, SSD_INNER), const2),
            pl.BlockSpec((SUBLANES, SSD_INNER), const2),
            pl.BlockSpec((SUBLANES, 1024), const2),
            pl.BlockSpec((SSD_PAIRS, SSD_STATE, LANES), lambda b, c: (0, 0, 0)),
        ],
        out_specs=[
            pl.BlockSpec((ln, SSD_INNER), lambda b, c: (b * nc + c, 0)),
            pl.BlockSpec((None, SUBLANES, SSD_INNER), lambda b, c: (b, 0, 0)),
            pl.BlockSpec((None, SUBLANES, 1024), lambda b, c: (b, 0, 0)),
            pl.BlockSpec((None, SSD_PAIRS, SSD_STATE, LANES), lambda b, c: (b, 0, 0, 0)),
        ],
        out_shape=[
            jax.ShapeDtypeStruct((nb * t_len, SSD_INNER), BF16),
            jax.ShapeDtypeStruct((nb, SUBLANES, SSD_INNER), F32),
            jax.ShapeDtypeStruct((nb, SUBLANES, 1024), F32),
            jax.ShapeDtypeStruct((nb, SSD_PAIRS, SSD_STATE, LANES), F32),
        ],
        scratch_shapes=[
            pltpu.VMEM((SUBLANES, SSD_INNER), F32),
            pltpu.VMEM((SUBLANES, 1024), F32),
            pltpu.VMEM((SSD_PAIRS, SSD_STATE, LANES), F32),
        ],
        compiler_params=_cparams(("arbitrary", "arbitrary")),
        name="ssd",
    )(proj, proj, proj, dtraw, cwx, cbx, cwb, cbb, dtb, alog, dexp, ng, tailx0, tailb0, s0)


def _merge_kernel(a_ref, y_ref, ga_ref, gb_ref, wl_ref, ws_ref, gbias_ref, o_ref):
    oa = jnp.dot(a_ref[...], wl_ref[...], preferred_element_type=F32)
    ob = jnp.dot(y_ref[...], ws_ref[...], preferred_element_type=F32)
    gbias = gbias_ref[...]
    g_a = _sigmoid(ga_ref[...] + gbias[0:1, :])
    g_b = _sigmoid(gb_ref[...] + gbias[1:2, :])
    o_ref[...] = (g_a * oa + g_b * ob).astype(BF16)


def _merge(a_lru, y_ssd, proj, wl, ws, gbias):
    m = a_lru.shape[0]
    tm = _pick(m, (1024, 512, 256, 128))
    tn = 1024
    return pl.pallas_call(
        _merge_kernel,
        grid=(m // tm, D_MODEL // tn),
        in_specs=[
            pl.BlockSpec((tm, LRU_WIDTH), lambda i, j: (i, 0)),
            pl.BlockSpec((tm, SSD_INNER), lambda i, j: (i, 0)),
            pl.BlockSpec((tm, tn), lambda i, j: (i, COL_GA // tn + j)),
            pl.BlockSpec((tm, tn), lambda i, j: (i, COL_GB // tn + j)),
            pl.BlockSpec((LRU_WIDTH, tn), lambda i, j: (0, j)),
            pl.BlockSpec((SSD_INNER, tn), lambda i, j: (0, j)),
            pl.BlockSpec((2, tn), lambda i, j: (0, j)),
        ],
        out_specs=pl.BlockSpec((tm, tn), lambda i, j: (i, j)),
        out_shape=jax.ShapeDtypeStruct((m, D_MODEL), BF16),
        compiler_params=_cparams(("parallel", "arbitrary")),
        name="merge",
    )(a_lru, y_ssd, proj, proj, wl, ws, gbias)


def _outproj_kernel(m_ref, x_ref, wo_ref, g_ref, wr_ref, br_ref, h_ref, xn_ref, te_ref, tw_ref):
    h2 = x_ref[...] + jnp.dot(m_ref[...], wo_ref[...], preferred_element_type=F32)
    h_ref[...] = h2
    ms = jnp.mean(h2 * h2, axis=-1, keepdims=True)
    xn = h2 * lax.rsqrt(ms + EPS) * g_ref[...]
    for c in range(ROW_CHUNKS):
        xn_ref[:, c, :] = xn[:, c * LANES:(c + 1) * LANES]
    logits = jnp.dot(xn, wr_ref[...], preferred_element_type=F32,
                     precision=lax.Precision.HIGHEST) + br_ref[...]
    tm = logits.shape[0]
    lane = lax.broadcasted_iota(jnp.int32, (tm, LANES), 1)
    vals = logits
    te = jnp.zeros((tm, LANES), jnp.int32)
    tw = jnp.zeros((tm, LANES), F32)
    m0 = None
    for k in range(TOP_K):
        mk = jnp.max(vals, axis=-1, keepdims=True)
        ik = jnp.min(jnp.where(vals == mk, lane, LANES), axis=-1, keepdims=True)
        if k == 0:
            m0 = mk
        te = jnp.where(lane == k, ik, te)
        tw = jnp.where(lane == k, jnp.exp(mk - m0), tw)
        vals = jnp.where(lane == ik, -jnp.inf, vals)
    tw = tw / jnp.sum(tw, axis=-1, keepdims=True)
    te_ref[...] = te
    tw_ref[...] = tw


def _outproj(merged, xrows, wo, g, wr, br):
    m = merged.shape[0]
    tm = _pick(m, (512, 256, 128))
    return pl.pallas_call(
        _outproj_kernel,
        grid=(m // tm,),
        in_specs=[
            pl.BlockSpec((tm, D_MODEL), lambda i: (i, 0)),
            pl.BlockSpec((tm, D_MODEL), lambda i: (i, 0)),
            pl.BlockSpec((D_MODEL, D_MODEL), lambda i: (0, 0)),
            pl.BlockSpec((1, D_MODEL), lambda i: (0, 0)),
            pl.BlockSpec((D_MODEL, LANES), lambda i: (0, 0)),
            pl.BlockSpec((1, LANES), lambda i: (0, 0)),
        ],
        out_specs=[
            pl.BlockSpec((tm, D_MODEL), lambda i: (i, 0)),
            pl.BlockSpec((tm, ROW_CHUNKS, LANES), lambda i: (i, 0, 0)),
            pl.BlockSpec((tm, LANES), lambda i: (i, 0)),
            pl.BlockSpec((tm, LANES), lambda i: (i, 0)),
        ],
        out_shape=[
            jax.ShapeDtypeStruct((m, D_MODEL), F32),
            jax.ShapeDtypeStruct((m, ROW_CHUNKS, LANES), F32),
            jax.ShapeDtypeStruct((m, LANES), jnp.int32),
            jax.ShapeDtypeStruct((m, LANES), F32),
        ],
        compiler_params=_cparams(("parallel",)),
        name="outproj",
    )(merged, xrows, wo, g, wr, br)


def _moe_kernel(be_ref, nv_ref, idx_ref, idxn_ref, xn_hbm, wg_ref, wu_ref, wd_ref, bg_ref, bu_ref, bd_ref,
                o_ref, xbuf, xb16, acc, sem, *, tm, nf):
    b = pl.program_id(0)
    f = pl.program_id(1)
    nv = nv_ref[0]

    def row_copy(idx_smem, r, slot):
        return pltpu.make_async_copy(xn_hbm.at[idx_smem[0, r]], xbuf.at[slot, r], sem.at[slot])

    def start_gather(idx_smem, slot):
        def body(r, carry):
            row_copy(idx_smem, r, slot).start()
            return carry
        lax.fori_loop(0, tm, body, 0)

    def wait_gather(slot):
        def body(r, carry):
            row_copy(idx_ref, 0, slot).wait()
            return carry
        lax.fori_loop(0, tm, body, 0)

    @pl.when((f == 0) & (b < nv))
    def _():
        slot = lax.rem(b, 2)

        @pl.when(b == 0)
        def _():
            start_gather(idx_ref, 0)

        wait_gather(slot)

        @pl.when(b + 1 < nv)
        def _():
            start_gather(idxn_ref, 1 - slot)

        for c in range(ROW_CHUNKS):
            xb16[:, c * LANES:(c + 1) * LANES] = xbuf[slot, :, c, :].astype(BF16)

    @pl.when(b < nv)
    def _():
        x = xb16[...]
        gate = jnp.dot(x, wg_ref[...], preferred_element_type=F32) + bg_ref[...]
        up = jnp.dot(x, wu_ref[...], preferred_element_type=F32) + bu_ref[...]
        gate = jnp.minimum(gate, SWIGLU_LIMIT)
        up = jnp.clip(up, -SWIGLU_LIMIT, SWIGLU_LIMIT)
        act = (up + 1.0) * gate * _sigmoid(SWIGLU_ALPHA * gate)
        part = jnp.dot(act.astype(BF16), wd_ref[...], preferred_element_type=F32)

        @pl.when(f == 0)
        def _():
            acc[...] = part

        @pl.when(f > 0)
        def _():
            acc[...] += part

        @pl.when(f == nf - 1)
        def _():
            res = acc[...] + bd_ref[...]
            for c in range(ROW_CHUNKS):
                o_ref[:, c, :] = res[:, c * LANES:(c + 1) * LANES]

    @pl.when((b >= nv) & (f == 0))
    def _():
        o_ref[...] = jnp.zeros(o_ref.shape, F32)


def _moe(block_e, nvalid, row_tok3, xn3, wgu, wd, bgu, bd, tm, tf):
    nb = row_tok3.shape[0]
    nf = D_FF // tf
    kern = functools.partial(_moe_kernel, tm=tm, nf=nf)

    def bclamp(b, nv):
        return jnp.minimum(b, nv[0] - 1)

    def fclamp(b, f, nv):
        return jnp.where(b < nv[0], f, nf - 1)

    grid_spec = pltpu.PrefetchScalarGridSpec(
        num_scalar_prefetch=2,
        grid=(nb, nf),
        in_specs=[
            pl.BlockSpec((None, 1, tm), lambda b, f, be, nv: (bclamp(b, nv), 0, 0),
                         memory_space=pltpu.SMEM),
            pl.BlockSpec((None, 1, tm), lambda b, f, be, nv: (bclamp(b + 1, nv), 0, 0),
                         memory_space=pltpu.SMEM),
            pl.BlockSpec(memory_space=pl.ANY),
            pl.BlockSpec((None, D_MODEL, tf), lambda b, f, be, nv: (be[bclamp(b, nv)], 0, fclamp(b, f, nv))),
            pl.BlockSpec((None, D_MODEL, tf),
                         lambda b, f, be, nv: (be[bclamp(b, nv)], 0, nf + fclamp(b, f, nv))),
            pl.BlockSpec((None, tf, D_MODEL), lambda b, f, be, nv: (be[bclamp(b, nv)], fclamp(b, f, nv), 0)),
            pl.BlockSpec((None, 1, tf), lambda b, f, be, nv: (be[bclamp(b, nv)], 0, fclamp(b, f, nv))),
            pl.BlockSpec((None, 1, tf), lambda b, f, be, nv: (be[bclamp(b, nv)], 0, nf + fclamp(b, f, nv))),
            pl.BlockSpec((None, 1, D_MODEL), lambda b, f, be, nv: (be[bclamp(b, nv)], 0, 0)),
        ],
        out_specs=pl.BlockSpec((tm, ROW_CHUNKS, LANES), lambda b, f, be, nv: (b, 0, 0)),
        scratch_shapes=[
            pltpu.VMEM((2, tm, ROW_CHUNKS, LANES), F32),
            pltpu.VMEM((tm, D_MODEL), BF16),
            pltpu.VMEM((tm, D_MODEL), F32),
            pltpu.SemaphoreType.DMA((2,)),
        ],
    )
    return pl.pallas_call(
        kern,
        grid_spec=grid_spec,
        out_shape=jax.ShapeDtypeStruct((nb * tm, ROW_CHUNKS, LANES), F32),
        compiler_params=_cparams(("arbitrary", "arbitrary")),
        name="moe",
    )(block_e, nvalid, row_tok3, row_tok3, xn3, wgu, wgu, wd, bgu, bgu, bd)


def _combine_kernel(idx_ref, idxn_ref, y_hbm, h_ref, tw_ref, g_ref, o_ref, buf, sem, *, tc, nsteps):
    i = pl.program_id(0)
    nrow = TOP_K * tc

    def row_copy(idx_smem, r, slot):
        return pltpu.make_async_copy(y_hbm.at[idx_smem[0, r]], buf.at[slot, r], sem.at[slot])

    def start_gather(idx_smem, slot):
        def body(r, carry):
            row_copy(idx_smem, r, slot).start()
            return carry
        lax.fori_loop(0, nrow, body, 0)

    slot = lax.rem(i, 2)

    @pl.when(i == 0)
    def _():
        start_gather(idx_ref, 0)

    def wbody(r, carry):
        row_copy(idx_ref, 0, slot).wait()
        return carry
    lax.fori_loop(0, nrow, wbody, 0)

    @pl.when(i + 1 < nsteps)
    def _():
        start_gather(idxn_ref, 1 - slot)

    tw = tw_ref[...]
    parts = []
    ssq = jnp.zeros((tc, 1), F32)
    for c in range(ROW_CHUNKS):
        v = h_ref[:, c * LANES:(c + 1) * LANES]
        for k in range(TOP_K):
            v = v + tw[:, k:k + 1] * buf[slot, pl.ds(k * tc, tc), c, :]
        parts.append(v)
        ssq = ssq + jnp.sum(v * v, axis=-1, keepdims=True)
    scale = lax.rsqrt(ssq * (1.0 / D_MODEL) + EPS)
    g = g_ref[...]
    for c in range(ROW_CHUNKS):
        o_ref[:, c * LANES:(c + 1) * LANES] = parts[c] * scale * g[:, c * LANES:(c + 1) * LANES]


def _combine(dest3, y3, h2, tw, g, tc):
    m = h2.shape[0]
    nsteps = m // tc
    kern = functools.partial(_combine_kernel, tc=tc, nsteps=nsteps)
    return pl.pallas_call(
        kern,
        grid=(nsteps,),
        in_specs=[
            pl.BlockSpec((None, 1, TOP_K * tc), lambda i: (i, 0, 0), memory_space=pltpu.SMEM),
            pl.BlockSpec((None, 1, TOP_K * tc), lambda i: (jnp.minimum(i + 1, nsteps - 1), 0, 0),
                         memory_space=pltpu.SMEM),
            pl.BlockSpec(memory_space=pl.ANY),
            pl.BlockSpec((tc, D_MODEL), lambda i: (i, 0)),
            pl.BlockSpec((tc, LANES), lambda i: (i, 0)),
            pl.BlockSpec((1, D_MODEL), lambda i: (0, 0)),
        ],
        out_specs=pl.BlockSpec((tc, D_MODEL), lambda i: (i, 0)),
        out_shape=jax.ShapeDtypeStruct((m, D_MODEL), F32),
        scratch_shapes=[
            pltpu.VMEM((2, TOP_K * tc, ROW_CHUNKS, LANES), F32),
            pltpu.SemaphoreType.DMA((2,)),
        ],
        compiler_params=_cparams(("arbitrary",)),
        name="combine",
    )(dest3, dest3, y3, h2, tw, g)


def _route(top_e, tm):
    m = top_e.shape[0]
    n_assign = m * TOP_K
    flat_e = top_e.reshape(-1)
    order = jnp.argsort(flat_e, stable=True).astype(jnp.int32)
    sorted_e = flat_e[order]
    counts = jnp.sum((flat_e[:, None] == jnp.arange(N_EXPERTS, dtype=jnp.int32)[None, :]).astype(jnp.int32),
                     axis=0)
    starts = jnp.cumsum(counts) - counts
    padded_counts = (counts + tm - 1) // tm * tm
    padded_end = jnp.cumsum(padded_counts)
    padded_start = padded_end - padded_counts
    dest_sorted = padded_start[sorted_e] + jnp.arange(n_assign, dtype=jnp.int32) - starts[sorted_e]
    n_blocks = n_assign // tm + N_EXPERTS
    row_tok = jnp.zeros((n_blocks * tm,), jnp.int32).at[dest_sorted].set(order // TOP_K)
    dest = jnp.zeros((n_assign,), jnp.int32).at[order].set(dest_sorted)
    block_e = jnp.minimum(
        jnp.searchsorted(padded_end, jnp.arange(n_blocks, dtype=jnp.int32) * tm, side='right'),
        N_EXPERTS - 1).astype(jnp.int32)
    nvalid = (padded_end[-1] // tm).astype(jnp.int32).reshape(1)
    return row_tok.reshape(n_blocks, 1, tm), dest.reshape(m, TOP_K), block_e, nvalid


def kernel(x, meta_tokens, norm_mix_g, w_in, gate_bias, lru_conv_w, lru_conv_b, lru_wa, lru_ba, lru_wi,
           lru_bi, lru_lambda, ssd_conv_w, ssd_conv_b, ssd_dt_bias, ssd_A_log, ssd_D, ssd_norm_g,
           w_proj_lru, w_proj_ssd, w_out, norm_moe_g, w_router, b_router, w_gate_up, b_gate_up, w_down,
           b_down, norm_final_g):
    bsz, t_len, _ = x.shape
    m = bsz * t_len
    xrows = x.reshape(m, D_MODEL)

    w = w_in[0]
    dt_lo = COL_BC + 1024
    w_main = jnp.concatenate([w[:, :dt_lo], w[:, dt_lo + SSD_HEADS:]], axis=1).astype(BF16)
    w_dt = jnp.pad(w[:, dt_lo:dt_lo + SSD_HEADS], ((0, 0), (0, LANES - SSD_HEADS))).astype(BF16)
    g_mix = norm_mix_g[0].reshape(1, D_MODEL)
    wai = jnp.concatenate([lru_wa[0], lru_wi[0]], axis=-1).astype(BF16)
    lru_p = (lru_conv_w[0], lru_conv_b[0].reshape(1, -1), wai, lru_ba[0].reshape(1, -1),
             lru_bi[0].reshape(1, -1), lru_lambda[0].reshape(1, -1))
    scw, scb = ssd_conv_w[0], ssd_conv_b[0].reshape(1, -1)
    ssd_p = (scw[:, :SSD_INNER], scb[:, :SSD_INNER], scw[:, SSD_INNER:], scb[:, SSD_INNER:],
             jnp.pad(ssd_dt_bias[0], (0, LANES - SSD_HEADS)).reshape(1, LANES),
             jnp.pad(ssd_A_log[0], (0, LANES - SSD_HEADS), constant_values=-1e30).reshape(1, LANES),
             jnp.repeat(ssd_D[0], SSD_HEAD_DIM).reshape(1, SSD_INNER),
             ssd_norm_g[0].reshape(1, SSD_INNER))
    wl = w_proj_lru[0].astype(BF16)
    ws = w_proj_ssd[0].astype(BF16)
    wo = w_out[0].astype(BF16)
    wr = jnp.pad(w_router[0], ((0, 0), (0, LANES - N_EXPERTS)))
    br = jnp.pad(b_router[0], (0, LANES - N_EXPERTS), constant_values=-1e30).reshape(1, LANES)
    wgu = w_gate_up[0].astype(BF16)
    wd = w_down[0].astype(BF16)
    bgu = b_gate_up[0].reshape(N_EXPERTS, 1, 2 * D_FF)
    bd = b_down[0].reshape(N_EXPERTS, 1, D_MODEL)

    meta_rows = jnp.concatenate(
        [jnp.zeros((SSD_CHUNK - N_META, D_MODEL), F32), meta_tokens.astype(F32)], axis=0)
    proj_m, dt_m = _inproj(meta_rows, g_mix, w_main, w_dt)
    z8 = jnp.zeros((SUBLANES, LRU_WIDTH), F32)
    _, lru_tail, lru_h = _lru(proj_m, 1, N_META, (SSD_CHUNK - N_META) // N_META, N_META, *lru_p,
                              z8, z8, True)
    _, tailx, tailb, s_meta = _ssd(proj_m, dt_m, 1, SSD_CHUNK, 0, ssd_p,
                                   jnp.zeros((SUBLANES, SSD_INNER), F32),
                                   jnp.zeros((SUBLANES, 1024), F32),
                                   jnp.zeros((SSD_PAIRS, SSD_STATE, LANES), F32),
                                   SSD_CHUNK - N_META)

    proj, dtraw = _inproj(xrows, g_mix, w_main, w_dt)
    tb = _pick(t_len, (256, 128))
    a_lru, _, _ = _lru(proj, bsz, t_len, 0, tb, *lru_p, lru_tail[0], lru_h[0], False)
    y_ssd, _, _, _ = _ssd(proj, dtraw, bsz, t_len, 0, ssd_p, tailx[0], tailb[0], s_meta[0], 0)
    merged = _merge(a_lru, y_ssd, proj, wl, ws, gate_bias[0])
    h2, xn3, te, tw = _outproj(merged, xrows, wo, norm_moe_g[0].reshape(1, D_MODEL), wr, br)

    tm_e = 512
    tc = 128
    row_tok3, dest, block_e, nvalid = _route(te[:, :TOP_K], tm_e)
    y3 = _moe(block_e, nvalid, row_tok3, xn3, wgu, wd, bgu, bd, tm_e, 512)
    dest3 = dest.reshape(m // tc, tc, TOP_K).transpose(0, 2, 1).reshape(m // tc, 1, TOP_K * tc)
    out = _combine(dest3, y3, h2, tw, norm_final_g.reshape(1, D_MODEL), tc)
    return out.reshape(bsz, t_len, D_MODEL)
```

```python
import functools

import jax
import jax.numpy as jnp
from jax import lax
from jax.experimental import pallas as pl
from jax.experimental.pallas import tpu as pltpu

F32 = jnp.float32
BF16 = jnp.bfloat16

D_MODEL = 2048
N_META = 16
EPS = 1e-6
CONV_W = 4
LRU_WIDTH = 1024
LRU_HEADS = 8
LRU_HEAD_DIM = 128
LRU_C = 8.0
SSD_INNER = 2048
SSD_HEAD_DIM = 64
SSD_HEADS = 32
SSD_GROUPS = 4
SSD_STATE = 128
SSD_CHUNK = 128
SSD_PAIRS = SSD_HEADS // 2
N_EXPERTS = 32
TOP_K = 4
D_FF = 2048
SWIGLU_LIMIT = 7.0
SWIGLU_ALPHA = 1.702

LANES = 128
SUBLANES = 8
VMEM_LIMIT = 56 * 1024 * 1024

COL_XA, COL_YA, COL_Z, COL_XS, COL_BC, COL_GA, COL_GB = 0, 1024, 2048, 4096, 6144, 7168, 9216
PROJ_W = 11264


def _cparams(sem):
    return pltpu.CompilerParams(dimension_semantics=sem, vmem_limit_bytes=VMEM_LIMIT)


def _sigmoid(v):
    return 1.0 / (1.0 + jnp.exp(-v))


def _silu(v):
    return v * _sigmoid(v)


def _softplus(v):
    return jnp.maximum(v, 0.0) + jnp.log1p(jnp.exp(-jnp.abs(v)))


def _gelu_tanh(v):
    return 0.5 * v * (1.0 + jnp.tanh(0.7978845608028654 * (v + 0.044715 * v * v * v)))


def _pick(n, candidates):
    for c in candidates:
        if n % c == 0:
            return c
    raise ValueError(f"no tile for {n}")


def _inproj_kernel(x_ref, g_ref, w_ref, wdt_ref, o_ref, odt_ref, xn_ref):
    @pl.when(pl.program_id(1) == 0)
    def _():
        x = x_ref[...]
        ms = jnp.mean(x * x, axis=-1, keepdims=True)
        xn = (x * lax.rsqrt(ms + EPS) * g_ref[...]).astype(BF16)
        xn_ref[...] = xn
        odt_ref[...] = jnp.dot(xn, wdt_ref[...], preferred_element_type=F32)

    o_ref[...] = jnp.dot(xn_ref[...], w_ref[...], preferred_element_type=F32)


def _inproj(xr, g, w_main, w_dt):
    rows = xr.shape[0]
    tm = _pick(rows, (1024, 512, 256, 128))
    tn = 1024
    return pl.pallas_call(
        _inproj_kernel,
        grid=(rows // tm, PROJ_W // tn),
        in_specs=[
            pl.BlockSpec((tm, D_MODEL), lambda i, j: (i, 0)),
            pl.BlockSpec((1, D_MODEL), lambda i, j: (0, 0)),
            pl.BlockSpec((D_MODEL, tn), lambda i, j: (0, j)),
            pl.BlockSpec((D_MODEL, LANES), lambda i, j: (0, 0)),
        ],
        out_specs=[
            pl.BlockSpec((tm, tn), lambda i, j: (i, j)),
            pl.BlockSpec((tm, LANES), lambda i, j: (i, 0)),
        ],
        out_shape=[
            jax.ShapeDtypeStruct((rows, PROJ_W), F32),
            jax.ShapeDtypeStruct((rows, LANES), F32),
        ],
        scratch_shapes=[pltpu.VMEM((tm, D_MODEL), BF16)],
        compiler_params=_cparams(("parallel", "arbitrary")),
        name="inproj",
    )(xr, g, w_main, w_dt)


def _causal_conv(x, tail, cw, cb):
    n = x.shape[0]
    ext = jnp.concatenate([tail, x], axis=0)
    y = cb + cw[CONV_W - 1:CONV_W] * x
    for k in range(1, CONV_W):
        y = y + cw[CONV_W - 1 - k:CONV_W - k] * pltpu.roll(ext, k, axis=0)[SUBLANES:SUBLANES + n]
    return y


def _lru_kernel(xa_ref, ya_ref, cw_ref, cb_ref, wai_ref, ba_ref, bi_ref, lam_ref, tail0_ref, h0_ref,
                o_ref, tail_out_ref, h_out_ref, tail_sc, h_sc, *, tb, seq_start):
    t = pl.program_id(1)

    @pl.when(t == 0)
    def _():
        tail_sc[...] = tail0_ref[...]
        h_sc[...] = h0_ref[...]

    xa = xa_ref[...]
    xc = _causal_conv(xa, tail_sc[...], cw_ref[...], cb_ref[...])
    tail_sc[...] = xa[tb - SUBLANES:tb]

    xcb = xc.astype(BF16)
    r_parts, i_parts = [], []
    for h in range(LRU_HEADS):
        g = jnp.dot(xcb[:, h * LRU_HEAD_DIM:(h + 1) * LRU_HEAD_DIM], wai_ref[h],
                    preferred_element_type=F32)
        r_parts.append(g[:, :LRU_HEAD_DIM])
        i_parts.append(g[:, LRU_HEAD_DIM:])
    r = _sigmoid(jnp.concatenate(r_parts, axis=1) + ba_ref[...])
    ig = _sigmoid(jnp.concatenate(i_parts, axis=1) + bi_ref[...])

    log_a = LRU_C * r * (-_softplus(-lam_ref[...]))
    a = jnp.exp(log_a)
    mult = jnp.sqrt(-jnp.tanh(log_a) * (a * a + 1.0))
    row = lax.broadcasted_iota(jnp.int32, (tb, LRU_WIDTH), 0)
    if seq_start:
        mult = jnp.where((row == 0) & (t == 0), 1.0, mult)
    u = mult * (ig * xc)

    s = 1
    while s < tb:
        a_sh = pltpu.roll(a, s, axis=0)
        u_sh = pltpu.roll(u, s, axis=0)
        valid = row >= s
        u = jnp.where(valid, a * u_sh, 0.0) + u
        a = jnp.where(valid, a * a_sh, a)
        s *= 2
    h = a * h_sc[0:1, :] + u
    h_last = jnp.broadcast_to(h[tb - 1:tb, :], (SUBLANES, LRU_WIDTH))
    h_sc[...] = h_last
    o_ref[...] = (_gelu_tanh(ya_ref[...]) * h).astype(BF16)
    tail_out_ref[...] = xa[tb - SUBLANES:tb]
    h_out_ref[...] = h_last


def _lru(proj, nb, t_len, row0_blocks, tb, cw, cb, wai, ba, bi, lam, tail0, h0, seq_start):
    nt = t_len // tb
    kern = functools.partial(_lru_kernel, tb=tb, seq_start=seq_start)
    per_b = (proj.shape[0] // nb) // tb
    return pl.pallas_call(
        kern,
        grid=(nb, nt),
        in_specs=[
            pl.BlockSpec((tb, LRU_WIDTH), lambda b, t: (b * per_b + row0_blocks + t, COL_XA // LRU_WIDTH)),
            pl.BlockSpec((tb, LRU_WIDTH), lambda b, t: (b * per_b + row0_blocks + t, COL_YA // LRU_WIDTH)),
            pl.BlockSpec((CONV_W, LRU_WIDTH), lambda b, t: (0, 0)),
            pl.BlockSpec((1, LRU_WIDTH), lambda b, t: (0, 0)),
            pl.BlockSpec((LRU_HEADS, LRU_HEAD_DIM, 2 * LRU_HEAD_DIM), lambda b, t: (0, 0, 0)),
            pl.BlockSpec((1, LRU_WIDTH), lambda b, t: (0, 0)),
            pl.BlockSpec((1, LRU_WIDTH), lambda b, t: (0, 0)),
            pl.BlockSpec((1, LRU_WIDTH), lambda b, t: (0, 0)),
            pl.BlockSpec((SUBLANES, LRU_WIDTH), lambda b, t: (0, 0)),
            pl.BlockSpec((SUBLANES, LRU_WIDTH), lambda b, t: (0, 0)),
        ],
        out_specs=[
            pl.BlockSpec((tb, LRU_WIDTH), lambda b, t: (b * nt + t, 0)),
            pl.BlockSpec((None, SUBLANES, LRU_WIDTH), lambda b, t: (b, 0, 0)),
            pl.BlockSpec((None, SUBLANES, LRU_WIDTH), lambda b, t: (b, 0, 0)),
        ],
        out_shape=[
            jax.ShapeDtypeStruct((nb * t_len, LRU_WIDTH), BF16),
            jax.ShapeDtypeStruct((nb, SUBLANES, LRU_WIDTH), F32),
            jax.ShapeDtypeStruct((nb, SUBLANES, LRU_WIDTH), F32),
        ],
        scratch_shapes=[pltpu.VMEM((SUBLANES, LRU_WIDTH), F32), pltpu.VMEM((SUBLANES, LRU_WIDTH), F32)],
        compiler_params=_cparams(("arbitrary", "arbitrary")),
        name="lru",
    )(proj, proj, cw, cb, wai, ba, bi, lam, tail0, h0)


def _ssd_kernel(xs_ref, bc_ref, z_ref, dt_ref, cwx_ref, cbx_ref, cwb_ref, cbb_ref, dtb_ref, alog_ref,
                dexp_ref, ng_ref, tailx0_ref, tailb0_ref, s0_ref,
                y_ref, tailx_out_ref, tailb_out_ref, s_out_ref,
                tailx_sc, tailb_sc, s_sc, *, first_valid):
    c = pl.program_id(1)
    ln = SSD_CHUNK

    @pl.when(c == 0)
    def _():
        tailx_sc[...] = tailx0_ref[...]
        tailb_sc[...] = tailb0_ref[...]
        s_sc[...] = s0_ref[...]

    xr = xs_ref[...]
    br = bc_ref[...]
    xs = _silu(_causal_conv(xr, tailx_sc[...], cwx_ref[...], cbx_ref[...]))
    bc = _silu(_causal_conv(br, tailb_sc[...], cwb_ref[...], cbb_ref[...]))
    tailx_sc[...] = xr[ln - SUBLANES:ln]
    tailb_sc[...] = br[ln - SUBLANES:ln]
    dt = _softplus(dt_ref[...] + dtb_ref[...])
    if first_valid:
        keep = lax.broadcasted_iota(jnp.int32, (ln, 1), 0) >= first_valid
        xs = jnp.where(keep, xs, 0.0)
        bc = jnp.where(keep, bc, 0.0)
        dt = jnp.where(keep, dt, 0.0)

    a_neg = -jnp.exp(alog_ref[...])
    da = dt * a_neg
    rr = lax.broadcasted_iota(jnp.int32, (ln, ln), 0)
    cc = lax.broadcasted_iota(jnp.int32, (ln, ln), 1)
    causal = rr >= cc
    cs = jnp.dot(causal.astype(F32), da, preferred_element_type=F32,
                 precision=lax.Precision.HIGHEST)
    cs_last = cs[ln - 1:ln, :]
    ecs = jnp.exp(cs)
    wdec = jnp.exp(cs_last - cs) * dt
    cs_t = cs.T
    dt_t = dt.T
    wdec_t = wdec.T
    dec_col = jnp.exp(cs_t[:, ln - 1:ln])

    lane = lax.broadcasted_iota(jnp.int32, (1, LANES), 1)
    left = lane < SSD_HEAD_DIM
    xsb = xs.astype(BF16)
    y_parts = []
    for g in range(SSD_GROUPS):
        bg = bc[:, g * SSD_STATE:(g + 1) * SSD_STATE]
        cg = bc[:, (SSD_GROUPS + g) * SSD_STATE:(SSD_GROUPS + g + 1) * SSD_STATE]
        cb_mat = lax.dot_general(cg.astype(BF16), bg.astype(BF16), (((1,), (1,)), ((), ())),
                                 preferred_element_type=F32)
        bg_t = bg.T
        for jj in range(SSD_PAIRS // SSD_GROUPS):
            j = g * (SSD_PAIRS // SSD_GROUPS) + jj
            xp = xsb[:, j * LANES:(j + 1) * LANES]
            sp = s_sc[j]
            rhs = jnp.concatenate([xp, sp.astype(BF16)], axis=0)
            y_h, s_h, d_h = [], [], []
            for h in (2 * j, 2 * j + 1):
                seg = cs[:, h:h + 1] - cs_t[h:h + 1, :]
                lmat = jnp.exp(jnp.where(causal, seg, -jnp.inf)) * dt_t[h:h + 1, :]
                m_h = (cb_mat * lmat).astype(BF16)
                c_off = (cg * ecs[:, h:h + 1]).astype(BF16)
                y_h.append(jnp.dot(jnp.concatenate([m_h, c_off], axis=1), rhs,
                                   preferred_element_type=F32))
                bw_t = (bg_t * wdec_t[h:h + 1, :]).astype(BF16)
                s_h.append(jnp.dot(bw_t, xp, preferred_element_type=F32))
                d_h.append(dec_col[h:h + 1, :])
            y_parts.append(jnp.where(left, y_h[0], y_h[1]))
            s_sc[j] = sp * jnp.where(left, d_h[0], d_h[1]) + jnp.where(left, s_h[0], s_h[1])

    y = jnp.concatenate(y_parts, axis=1) + xs * dexp_ref[...]
    y = y * _silu(z_ref[...])
    gw = SSD_INNER // SSD_GROUPS
    outs = []
    for g in range(SSD_GROUPS):
        yg = y[:, g * gw:(g + 1) * gw]
        outs.append(yg * lax.rsqrt(jnp.mean(yg * yg, axis=-1, keepdims=True) + EPS))
    y_ref[...] = (jnp.concatenate(outs, axis=1) * ng_ref[...]).astype(BF16)
    tailx_out_ref[...] = xr[ln - SUBLANES:ln]
    tailb_out_ref[...] = br[ln - SUBLANES:ln]
    s_out_ref[...] = s_sc[...]


def _ssd(proj, dtraw, nb, t_len, row0_blocks, params, tailx0, tailb0, s0, first_valid):
    ln = SSD_CHUNK
    nc = t_len // ln
    per_b = (proj.shape[0] // nb) // ln
    cwx, cbx, cwb, cbb, dtb, alog, dexp, ng = params
    kern = functools.partial(_ssd_kernel, first_valid=first_valid)
    const2 = lambda b, c: (0, 0)
    rowi = lambda b, c: b * per_b + row0_blocks + c
    return pl.pallas_call(
        kern,
        grid=(nb, nc),
        in_specs=[
            pl.BlockSpec((ln, SSD_INNER), lambda b, c: (rowi(b, c), COL_XS // SSD_INNER)),
            pl.BlockSpec((ln, 1024), lambda b, c: (rowi(b, c), COL_BC // 1024)),
            pl.BlockSpec((ln, SSD_INNER), lambda b, c: (rowi(b, c), COL_Z // SSD_INNER)),
            pl.BlockSpec((ln, LANES), lambda b, c: (rowi(b, c), 0)),
            pl.BlockSpec((CONV_W, SSD_INNER), const2),
            pl.BlockSpec((1, SSD_INNER), const2),
            pl.BlockSpec((CONV_W, 1024), const2),
            pl.BlockSpec((1, 1024), const2),
            pl.BlockSpec((1, LANES), const2),
            pl.BlockSpec((1, LANES), const2),
            pl.BlockSpec((1, SSD_INNER), const2),
            pl.BlockSpec((1, SSD_INNER), const2),
            pl.BlockSpec((SUBLANES, SSD_INNER), const2),
            pl.BlockSpec((SUBLANES, 1024), const2),
            pl.BlockSpec((SSD_PAIRS, SSD_STATE, LANES), lambda b, c: (0, 0, 0)),
        ],
        out_specs=[
            pl.BlockSpec((ln, SSD_INNER), lambda b, c: (b * nc + c, 0)),
            pl.BlockSpec((None, SUBLANES, SSD_INNER), lambda b, c: (b, 0, 0)),
            pl.BlockSpec((None, SUBLANES, 1024), lambda b, c: (b, 0, 0)),
            pl.BlockSpec((None, SSD_PAIRS, SSD_STATE, LANES), lambda b, c: (b, 0, 0, 0)),
        ],
        out_shape=[
            jax.ShapeDtypeStruct((nb * t_len, SSD_INNER), BF16),
            jax.ShapeDtypeStruct((nb, SUBLANES, SSD_INNER), F32),
            jax.ShapeDtypeStruct((nb, SUBLANES, 1024), F32),
            jax.ShapeDtypeStruct((nb, SSD_PAIRS, SSD_STATE, LANES), F32),
        ],
        scratch_shapes=[
            pltpu.VMEM((SUBLANES, SSD_INNER), F32),
            pltpu.VMEM((SUBLANES, 1024), F32),
            pltpu.VMEM((SSD_PAIRS, SSD_STATE, LANES), F32),
        ],
        compiler_params=_cparams(("arbitrary", "arbitrary")),
        name="ssd",
    )(proj, proj, proj, dtraw, cwx, cbx, cwb, cbb, dtb, alog, dexp, ng, tailx0, tailb0, s0)


def _merge_kernel(a_ref, y_ref, ga_ref, gb_ref, wl_ref, ws_ref, gbias_ref, o_ref):
    oa = jnp.dot(a_ref[...], wl_ref[...], preferred_element_type=F32)
    ob = jnp.dot(y_ref[...], ws_ref[...], preferred_element_type=F32)
    gbias = gbias_ref[...]
    g_a = _sigmoid(ga_ref[...] + gbias[0:1, :])
    g_b = _sigmoid(gb_ref[...] + gbias[1:2, :])
    o_ref[...] = (g_a * oa + g_b * ob).astype(BF16)


def _merge(a_lru, y_ssd, proj, wl, ws, gbias):
    m = a_lru.shape[0]
    tm = _pick(m, (1024, 512, 256, 128))
    tn = 1024
    return pl.pallas_call(
        _merge_kernel,
        grid=(m // tm, D_MODEL // tn),
        in_specs=[
            pl.BlockSpec((tm, LRU_WIDTH), lambda i, j: (i, 0)),
            pl.BlockSpec((tm, SSD_INNER), lambda i, j: (i, 0)),
            pl.BlockSpec((tm, tn), lambda i, j: (i, COL_GA // tn + j)),
            pl.BlockSpec((tm, tn), lambda i, j: (i, COL_GB // tn + j)),
            pl.BlockSpec((LRU_WIDTH, tn), lambda i, j: (0, j)),
            pl.BlockSpec((SSD_INNER, tn), lambda i, j: (0, j)),
            pl.BlockSpec((2, tn), lambda i, j: (0, j)),
        ],
        out_specs=pl.BlockSpec((tm, tn), lambda i, j: (i, j)),
        out_shape=jax.ShapeDtypeStruct((m, D_MODEL), BF16),
        compiler_params=_cparams(("parallel", "arbitrary")),
        name="merge",
    )(a_lru, y_ssd, proj, proj, wl, ws, gbias)


def _outproj_kernel(m_ref, x_ref, wo_ref, g_ref, wr_ref, br_ref, h_ref, xn_ref, te_ref, tw_ref):
    h2 = x_ref[...] + jnp.dot(m_ref[...], wo_ref[...], preferred_element_type=F32)
    h_ref[...] = h2
    ms = jnp.mean(h2 * h2, axis=-1, keepdims=True)
    xn = h2 * lax.rsqrt(ms + EPS) * g_ref[...]
    xn_ref[...] = xn
    tm = xn.shape[0]
    x_hi = xn.astype(BF16)
    x_lo = (xn - x_hi.astype(F32)).astype(BF16)
    prod = jnp.dot(jnp.concatenate([x_hi, x_lo], axis=0), wr_ref[...], preferred_element_type=F32)
    top, bot = prod[:tm], prod[tm:]
    lane = lax.broadcasted_iota(jnp.int32, (tm, LANES), 1)
    logits = top + pltpu.roll(top, LANES - N_EXPERTS, axis=1) + bot + br_ref[...]
    vals = jnp.where(lane < N_EXPERTS, logits, -1e30)
    te = jnp.zeros((tm, LANES), jnp.int32)
    tw = jnp.zeros((tm, LANES), F32)
    m0 = None
    for k in range(TOP_K):
        mk = jnp.max(vals, axis=-1, keepdims=True)
        ik = jnp.min(jnp.where(vals == mk, lane, LANES), axis=-1, keepdims=True)
        if k == 0:
            m0 = mk
        te = jnp.where(lane == k, ik, te)
        tw = jnp.where(lane == k, jnp.exp(mk - m0), tw)
        vals = jnp.where(lane == ik, -jnp.inf, vals)
    tw = tw / jnp.sum(tw, axis=-1, keepdims=True)
    te_ref[...] = te
    tw_ref[...] = tw


def _outproj(merged, xrows, wo, g, wr, br):
    m = merged.shape[0]
    tm = _pick(m, (512, 256, 128))
    return pl.pallas_call(
        _outproj_kernel,
        grid=(m // tm,),
        in_specs=[
            pl.BlockSpec((tm, D_MODEL), lambda i: (i, 0)),
            pl.BlockSpec((tm, D_MODEL), lambda i: (i, 0)),
            pl.BlockSpec((D_MODEL, D_MODEL), lambda i: (0, 0)),
            pl.BlockSpec((1, D_MODEL), lambda i: (0, 0)),
            pl.BlockSpec((D_MODEL, LANES), lambda i: (0, 0)),
            pl.BlockSpec((1, LANES), lambda i: (0, 0)),
        ],
        out_specs=[
            pl.BlockSpec((tm, D_MODEL), lambda i: (i, 0)),
            pl.BlockSpec((tm, D_MODEL), lambda i: (i, 0)),
            pl.BlockSpec((tm, LANES), lambda i: (i, 0)),
            pl.BlockSpec((tm, LANES), lambda i: (i, 0)),
        ],
        out_shape=[
            jax.ShapeDtypeStruct((m, D_MODEL), F32),
            jax.ShapeDtypeStruct((m, D_MODEL), F32),
            jax.ShapeDtypeStruct((m, LANES), jnp.int32),
            jax.ShapeDtypeStruct((m, LANES), F32),
        ],
        compiler_params=_cparams(("parallel",)),
        name="outproj",
    )(merged, xrows, wo, g, wr, br)


def _moe_kernel(be_ref, nv_ref, idx_ref, idxn_ref, xn_hbm, wg_ref, wu_ref, wd_ref, bg_ref, bu_ref, bd_ref,
                o_ref, xbuf, xb16, sem, *, tm):
    b = pl.program_id(0)
    f = pl.program_id(1)
    nv = nv_ref[0]

    def start_gather(idx_smem, slot):
        def body(r, carry):
            pltpu.make_async_copy(xn_hbm.at[pl.ds(idx_smem[0, r], 1), :],
                                  xbuf.at[slot, pl.ds(r, 1), :], sem.at[slot]).start()
            return carry
        lax.fori_loop(0, tm, body, 0, unroll=8)

    def wait_gather(slot):
        pltpu.make_async_copy(xn_hbm.at[pl.ds(0, tm), :], xbuf.at[slot], sem.at[slot]).wait()

    @pl.when((f == 0) & (b < nv))
    def _():
        slot = lax.rem(b, 2)

        @pl.when(b == 0)
        def _():
            start_gather(idx_ref, 0)

        wait_gather(slot)

        @pl.when(b + 1 < nv)
        def _():
            start_gather(idxn_ref, 1 - slot)

        xb16[...] = xbuf[slot].astype(BF16)

    @pl.when(b < nv)
    def _():
        x = xb16[...]
        gate = jnp.dot(x, wg_ref[...], preferred_element_type=F32) + bg_ref[...]
        up = jnp.dot(x, wu_ref[...], preferred_element_type=F32) + bu_ref[...]
        gate = jnp.minimum(gate, SWIGLU_LIMIT)
        up = jnp.clip(up, -SWIGLU_LIMIT, SWIGLU_LIMIT)
        act = (up + 1.0) * gate * _sigmoid(SWIGLU_ALPHA * gate)
        part = jnp.dot(act.astype(BF16), wd_ref[...], preferred_element_type=F32)

        @pl.when(f == 0)
        def _():
            o_ref[...] = part + bd_ref[...]

        @pl.when(f > 0)
        def _():
            o_ref[...] += part

    @pl.when((b >= nv) & (f == 0))
    def _():
        o_ref[...] = jnp.zeros(o_ref.shape, F32)


def _moe(block_e, nvalid, row_tok3, xn, wgu, wd, bgu, bd, tm, tf):
    nb = row_tok3.shape[0]
    nf = D_FF // tf
    kern = functools.partial(_moe_kernel, tm=tm)

    def bclamp(b, nv):
        return jnp.minimum(b, nv[0] - 1)

    def fclamp(b, f, nv):
        return jnp.where(b < nv[0], f, nf - 1)

    grid_spec = pltpu.PrefetchScalarGridSpec(
        num_scalar_prefetch=2,
        grid=(nb, nf),
        in_specs=[
            pl.BlockSpec((None, 1, tm), lambda b, f, be, nv: (bclamp(b, nv), 0, 0),
                         memory_space=pltpu.SMEM),
            pl.BlockSpec((None, 1, tm), lambda b, f, be, nv: (bclamp(b + 1, nv), 0, 0),
                         memory_space=pltpu.SMEM),
            pl.BlockSpec(memory_space=pl.ANY),
            pl.BlockSpec((None, D_MODEL, tf), lambda b, f, be, nv: (be[bclamp(b, nv)], 0, fclamp(b, f, nv))),
            pl.BlockSpec((None, D_MODEL, tf),
                         lambda b, f, be, nv: (be[bclamp(b, nv)], 0, nf + fclamp(b, f, nv))),
            pl.BlockSpec((None, tf, D_MODEL), lambda b, f, be, nv: (be[bclamp(b, nv)], fclamp(b, f, nv), 0)),
            pl.BlockSpec((None, 1, tf), lambda b, f, be, nv: (be[bclamp(b, nv)], 0, fclamp(b, f, nv))),
            pl.BlockSpec((None, 1, tf), lambda b, f, be, nv: (be[bclamp(b, nv)], 0, nf + fclamp(b, f, nv))),
            pl.BlockSpec((None, 1, D_MODEL), lambda b, f, be, nv: (be[bclamp(b, nv)], 0, 0)),
        ],
        out_specs=pl.BlockSpec((tm, D_MODEL), lambda b, f, be, nv: (b, 0)),
        scratch_shapes=[
            pltpu.VMEM((2, tm, D_MODEL), F32),
            pltpu.VMEM((tm, D_MODEL), BF16),
            pltpu.SemaphoreType.DMA((2,)),
        ],
    )
    return pl.pallas_call(
        kern,
        grid_spec=grid_spec,
        out_shape=jax.ShapeDtypeStruct((nb * tm, D_MODEL), F32),
        compiler_params=_cparams(("arbitrary", "arbitrary")),
        name="moe",
    )(block_e, nvalid, row_tok3, row_tok3, xn, wgu, wgu, wd, bgu, bgu, bd)


def _combine_kernel(idx_ref, idxn_ref, y_hbm, h_ref, tw_ref, g_ref, o_ref, buf, sem, *, tc, nsteps):
    i = pl.program_id(0)
    nrow = TOP_K * tc

    def start_gather(idx_smem, slot):
        def body(r, carry):
            pltpu.make_async_copy(y_hbm.at[pl.ds(idx_smem[0, r], 1), :],
                                  buf.at[slot, pl.ds(r, 1), :], sem.at[slot]).start()
            return carry
        lax.fori_loop(0, nrow, body, 0, unroll=8)

    slot = lax.rem(i, 2)

    @pl.when(i == 0)
    def _():
        start_gather(idx_ref, 0)

    pltpu.make_async_copy(y_hbm.at[pl.ds(0, nrow), :], buf.at[slot], sem.at[slot]).wait()

    @pl.when(i + 1 < nsteps)
    def _():
        start_gather(idxn_ref, 1 - slot)

    tw = tw_ref[...]
    v = h_ref[...]
    for k in range(TOP_K):
        v = v + tw[:, k:k + 1] * buf[slot, pl.ds(k * tc, tc), :]
    ms = jnp.mean(v * v, axis=-1, keepdims=True)
    o_ref[...] = v * lax.rsqrt(ms + EPS) * g_ref[...]


def _combine(dest3, y, h2, tw, g, tc):
    m = h2.shape[0]
    nsteps = m // tc
    kern = functools.partial(_combine_kernel, tc=tc, nsteps=nsteps)
    return pl.pallas_call(
        kern,
        grid=(nsteps,),
        in_specs=[
            pl.BlockSpec((None, 1, TOP_K * tc), lambda i: (i, 0, 0), memory_space=pltpu.SMEM),
            pl.BlockSpec((None, 1, TOP_K * tc), lambda i: (jnp.minimum(i + 1, nsteps - 1), 0, 0),
                         memory_space=pltpu.SMEM),
            pl.BlockSpec(memory_space=pl.ANY),
            pl.BlockSpec((tc, D_MODEL), lambda i: (i, 0)),
            pl.BlockSpec((tc, LANES), lambda i: (i, 0)),
            pl.BlockSpec((1, D_MODEL), lambda i: (0, 0)),
        ],
        out_specs=pl.BlockSpec((tc, D_MODEL), lambda i: (i, 0)),
        out_shape=jax.ShapeDtypeStruct((m, D_MODEL), F32),
        scratch_shapes=[
            pltpu.VMEM((2, TOP_K * tc, D_MODEL), F32),
            pltpu.SemaphoreType.DMA((2,)),
        ],
        compiler_params=_cparams(("arbitrary",)),
        name="combine",
    )(dest3, dest3, y, h2, tw, g)


def _route(top_e, tm):
    m = top_e.shape[0]
    n_assign = m * TOP_K
    flat_e = top_e.reshape(-1)
    order = jnp.argsort(flat_e, stable=True).astype(jnp.int32)
    rank = jnp.argsort(order).astype(jnp.int32)
    counts = jnp.sum((flat_e[:, None] == jnp.arange(N_EXPERTS, dtype=jnp.int32)[None, :]).astype(jnp.int32),
                     axis=0)
    starts = jnp.cumsum(counts) - counts
    padded_counts = (counts + tm - 1) // tm * tm
    padded_end = jnp.cumsum(padded_counts)
    padded_start = padded_end - padded_counts
    shift = padded_start - starts
    dest = rank + shift[flat_e]
    n_blocks = n_assign // tm + N_EXPERTS
    block_row0 = jnp.arange(n_blocks, dtype=jnp.int32) * tm
    block_e = jnp.minimum(jnp.sum((block_row0[:, None] >= padded_end[None, :]).astype(jnp.int32), axis=1),
                          N_EXPERTS - 1).astype(jnp.int32)
    off = jnp.clip(block_row0 - shift[block_e], 0, n_assign)
    ctok = jnp.concatenate([order // TOP_K, jnp.zeros((tm,), jnp.int32)])
    row_tok = jax.vmap(lambda o: lax.dynamic_slice(ctok, (o,), (tm,)))(off)
    nvalid = (padded_end[-1] // tm).astype(jnp.int32).reshape(1)
    return row_tok.reshape(n_blocks, 1, tm), dest.reshape(m, TOP_K), block_e, nvalid


def kernel(x, meta_tokens, norm_mix_g, w_in, gate_bias, lru_conv_w, lru_conv_b, lru_wa, lru_ba, lru_wi,
           lru_bi, lru_lambda, ssd_conv_w, ssd_conv_b, ssd_dt_bias, ssd_A_log, ssd_D, ssd_norm_g,
           w_proj_lru, w_proj_ssd, w_out, norm_moe_g, w_router, b_router, w_gate_up, b_gate_up, w_down,
           b_down, norm_final_g):
    bsz, t_len, _ = x.shape
    m = bsz * t_len
    xrows = x.reshape(m, D_MODEL)

    w = w_in[0]
    dt_lo = COL_BC + 1024
    w_main = jnp.concatenate([w[:, :dt_lo], w[:, dt_lo + SSD_HEADS:]], axis=1).astype(BF16)
    w_dt = jnp.pad(w[:, dt_lo:dt_lo + SSD_HEADS], ((0, 0), (0, LANES - SSD_HEADS))).astype(BF16)
    g_mix = norm_mix_g[0].reshape(1, D_MODEL)
    wai = jnp.concatenate([lru_wa[0], lru_wi[0]], axis=-1).astype(BF16)
    lru_p = (lru_conv_w[0], lru_conv_b[0].reshape(1, -1), wai, lru_ba[0].reshape(1, -1),
             lru_bi[0].reshape(1, -1), lru_lambda[0].reshape(1, -1))
    scw, scb = ssd_conv_w[0], ssd_conv_b[0].reshape(1, -1)
    ssd_p = (scw[:, :SSD_INNER], scb[:, :SSD_INNER], scw[:, SSD_INNER:], scb[:, SSD_INNER:],
             jnp.pad(ssd_dt_bias[0], (0, LANES - SSD_HEADS)).reshape(1, LANES),
             jnp.pad(ssd_A_log[0], (0, LANES - SSD_HEADS), constant_values=-1e30).reshape(1, LANES),
             jnp.repeat(ssd_D[0], SSD_HEAD_DIM).reshape(1, SSD_INNER),
             ssd_norm_g[0].reshape(1, SSD_INNER))
    wl = w_proj_lru[0].astype(BF16)
    ws = w_proj_ssd[0].astype(BF16)
    wo = w_out[0].astype(BF16)
    wr_hi = w_router[0].astype(BF16)
    wr_lo = (w_router[0] - wr_hi.astype(F32)).astype(BF16)
    wr = jnp.pad(jnp.concatenate([wr_hi, wr_lo], axis=1), ((0, 0), (0, LANES - 2 * N_EXPERTS)))
    br = jnp.pad(b_router[0], (0, LANES - N_EXPERTS)).reshape(1, LANES)
    wgu = w_gate_up[0].astype(BF16)
    wd = w_down[0].astype(BF16)
    bgu = b_gate_up[0].reshape(N_EXPERTS, 1, 2 * D_FF)
    bd = b_down[0].reshape(N_EXPERTS, 1, D_MODEL)

    meta_rows = jnp.concatenate(
        [jnp.zeros((SSD_CHUNK - N_META, D_MODEL), F32), meta_tokens.astype(F32)], axis=0)
    proj_m, dt_m = _inproj(meta_rows, g_mix, w_main, w_dt)
    z8 = jnp.zeros((SUBLANES, LRU_WIDTH), F32)
    _, lru_tail, lru_h = _lru(proj_m, 1, N_META, (SSD_CHUNK - N_META) // N_META, N_META, *lru_p,
                              z8, z8, True)
    _, tailx, tailb, s_meta = _ssd(proj_m, dt_m, 1, SSD_CHUNK, 0, ssd_p,
                                   jnp.zeros((SUBLANES, SSD_INNER), F32),
                                   jnp.zeros((SUBLANES, 1024), F32),
                                   jnp.zeros((SSD_PAIRS, SSD_STATE, LANES), F32),
                                   SSD_CHUNK - N_META)

    proj, dtraw = _inproj(xrows, g_mix, w_main, w_dt)
    tb = _pick(t_len, (256, 128))
    a_lru, _, _ = _lru(proj, bsz, t_len, 0, tb, *lru_p, lru_tail[0], lru_h[0], False)
    y_ssd, _, _, _ = _ssd(proj, dtraw, bsz, t_len, 0, ssd_p, tailx[0], tailb[0], s_meta[0], 0)
    merged = _merge(a_lru, y_ssd, proj, wl, ws, gate_bias[0])
    h2, xn, te, tw = _outproj(merged, xrows, wo, norm_moe_g[0].reshape(1, D_MODEL), wr, br)

    tm_e = 512
    tc = 128
    row_tok3, dest, block_e, nvalid = _route(te[:, :TOP_K], tm_e)
    y_rows = _moe(block_e, nvalid, row_tok3, xn, wgu, wd, bgu, bd, tm_e, 512)
    dest3 = dest.reshape(m // tc, tc, TOP_K).transpose(0, 2, 1).reshape(m // tc, 1, TOP_K * tc)
    out = _combine(dest3, y_rows, h2, tw, norm_final_g.reshape(1, D_MODEL), tc)
    return out.reshape(bsz, t_len, D_MODEL)
```

```python
import functools

import jax
import jax.numpy as jnp
from jax import lax
from jax.experimental import pallas as pl
from jax.experimental.pallas import tpu as pltpu

F32 = jnp.float32
BF16 = jnp.bfloat16

D_MODEL = 2048
N_META = 16
EPS = 1e-6
CONV_W = 4
LRU_WIDTH = 1024
LRU_HEADS = 8
LRU_HEAD_DIM = 128
LRU_C = 8.0
SSD_INNER = 2048
SSD_HEAD_DIM = 64
SSD_HEADS = 32
SSD_GROUPS = 4
SSD_STATE = 128
SSD_CHUNK = 128
SSD_PAIRS = SSD_HEADS // 2
N_EXPERTS = 32
TOP_K = 4
D_FF = 2048
SWIGLU_LIMIT = 7.0
SWIGLU_ALPHA = 1.702

LANES = 128
SUBLANES = 8
VMEM_LIMIT = 56 * 1024 * 1024

COL_XA, COL_YA, COL_Z, COL_XS, COL_BC, COL_GA, COL_GB = 0, 1024, 2048, 4096, 6144, 7168, 9216
PROJ_W = 11264


def _cparams(sem):
    return pltpu.CompilerParams(dimension_semantics=sem, vmem_limit_bytes=VMEM_LIMIT)


def _sigmoid(v):
    return 1.0 / (1.0 + jnp.exp(-v))


def _silu(v):
    return v * _sigmoid(v)


def _softplus(v):
    return jnp.maximum(v, 0.0) + jnp.log1p(jnp.exp(-jnp.abs(v)))


def _gelu_tanh(v):
    return 0.5 * v * (1.0 + jnp.tanh(0.7978845608028654 * (v + 0.044715 * v * v * v)))


def _pick(n, candidates):
    for c in candidates:
        if n % c == 0:
            return c
    raise ValueError(f"no tile for {n}")


def _inproj_kernel(x_ref, g_ref, w_ref, wdt_ref, o_ref, odt_ref, xn_ref):
    @pl.when(pl.program_id(1) == 0)
    def _():
        x = x_ref[...]
        ms = jnp.mean(x * x, axis=-1, keepdims=True)
        xn = (x * lax.rsqrt(ms + EPS) * g_ref[...]).astype(BF16)
        xn_ref[...] = xn
        odt_ref[...] = jnp.dot(xn, wdt_ref[...], preferred_element_type=F32)

    o_ref[...] = jnp.dot(xn_ref[...], w_ref[...], preferred_element_type=F32)


def _inproj(xr, g, w_main, w_dt):
    rows = xr.shape[0]
    tm = _pick(rows, (1024, 512, 256, 128))
    tn = 1024
    return pl.pallas_call(
        _inproj_kernel,
        grid=(rows // tm, PROJ_W // tn),
        in_specs=[
            pl.BlockSpec((tm, D_MODEL), lambda i, j: (i, 0)),
            pl.BlockSpec((1, D_MODEL), lambda i, j: (0, 0)),
            pl.BlockSpec((D_MODEL, tn), lambda i, j: (0, j)),
            pl.BlockSpec((D_MODEL, LANES), lambda i, j: (0, 0)),
        ],
        out_specs=[
            pl.BlockSpec((tm, tn), lambda i, j: (i, j)),
            pl.BlockSpec((tm, LANES), lambda i, j: (i, 0)),
        ],
        out_shape=[
            jax.ShapeDtypeStruct((rows, PROJ_W), F32),
            jax.ShapeDtypeStruct((rows, LANES), F32),
        ],
        scratch_shapes=[pltpu.VMEM((tm, D_MODEL), BF16)],
        compiler_params=_cparams(("parallel", "arbitrary")),
        name="inproj",
    )(xr, g, w_main, w_dt)


def _causal_conv(x, tail, cw, cb):
    n = x.shape[0]
    ext = jnp.concatenate([tail, x], axis=0)
    y = cb + cw[CONV_W - 1:CONV_W] * x
    for k in range(1, CONV_W):
        y = y + cw[CONV_W - 1 - k:CONV_W - k] * pltpu.roll(ext, k, axis=0)[SUBLANES:SUBLANES + n]
    return y


def _lru_kernel(xa_ref, ya_ref, cw_ref, cb_ref, wai_ref, ba_ref, bi_ref, lam_ref, tail0_ref, h0_ref,
                o_ref, tail_out_ref, h_out_ref, tail_sc, h_sc, *, tb, seq_start):
    t = pl.program_id(1)

    @pl.when(t == 0)
    def _():
        tail_sc[...] = tail0_ref[...]
        h_sc[...] = h0_ref[...]

    xa = xa_ref[...]
    xc = _causal_conv(xa, tail_sc[...], cw_ref[...], cb_ref[...])
    tail_sc[...] = xa[tb - SUBLANES:tb]

    xcb = xc.astype(BF16)
    r_parts, i_parts = [], []
    for h in range(LRU_HEADS):
        g = jnp.dot(xcb[:, h * LRU_HEAD_DIM:(h + 1) * LRU_HEAD_DIM], wai_ref[h],
                    preferred_element_type=F32)
        r_parts.append(g[:, :LRU_HEAD_DIM])
        i_parts.append(g[:, LRU_HEAD_DIM:])
    r = _sigmoid(jnp.concatenate(r_parts, axis=1) + ba_ref[...])
    ig = _sigmoid(jnp.concatenate(i_parts, axis=1) + bi_ref[...])

    log_a = LRU_C * r * (-_softplus(-lam_ref[...]))
    a = jnp.exp(log_a)
    mult = jnp.sqrt(-jnp.tanh(log_a) * (a * a + 1.0))
    row = lax.broadcasted_iota(jnp.int32, (tb, LRU_WIDTH), 0)
    if seq_start:
        mult = jnp.where((row == 0) & (t == 0), 1.0, mult)
    u = mult * (ig * xc)

    s = 1
    while s < tb:
        a_sh = pltpu.roll(a, s, axis=0)
        u_sh = pltpu.roll(u, s, axis=0)
        valid = row >= s
        u = jnp.where(valid, a * u_sh, 0.0) + u
        a = jnp.where(valid, a * a_sh, a)
        s *= 2
    h = a * h_sc[0:1, :] + u
    h_last = jnp.broadcast_to(h[tb - 1:tb, :], (SUBLANES, LRU_WIDTH))
    h_sc[...] = h_last
    o_ref[...] = (_gelu_tanh(ya_ref[...]) * h).astype(BF16)
    tail_out_ref[...] = xa[tb - SUBLANES:tb]
    h_out_ref[...] = h_last


def _lru(proj, nb, t_len, row0_blocks, tb, cw, cb, wai, ba, bi, lam, tail0, h0, seq_start):
    nt = t_len // tb
    kern = functools.partial(_lru_kernel, tb=tb, seq_start=seq_start)
    per_b = (proj.shape[0] // nb) // tb
    return pl.pallas_call(
        kern,
        grid=(nb, nt),
        in_specs=[
            pl.BlockSpec((tb, LRU_WIDTH), lambda b, t: (b * per_b + row0_blocks + t, COL_XA // LRU_WIDTH)),
            pl.BlockSpec((tb, LRU_WIDTH), lambda b, t: (b * per_b + row0_blocks + t, COL_YA // LRU_WIDTH)),
            pl.BlockSpec((CONV_W, LRU_WIDTH), lambda b, t: (0, 0)),
            pl.BlockSpec((1, LRU_WIDTH), lambda b, t: (0, 0)),
            pl.BlockSpec((LRU_HEADS, LRU_HEAD_DIM, 2 * LRU_HEAD_DIM), lambda b, t: (0, 0, 0)),
            pl.BlockSpec((1, LRU_WIDTH), lambda b, t: (0, 0)),
            pl.BlockSpec((1, LRU_WIDTH), lambda b, t: (0, 0)),
            pl.BlockSpec((1, LRU_WIDTH), lambda b, t: (0, 0)),
            pl.BlockSpec((SUBLANES, LRU_WIDTH), lambda b, t: (0, 0)),
            pl.BlockSpec((SUBLANES, LRU_WIDTH), lambda b, t: (0, 0)),
        ],
        out_specs=[
            pl.BlockSpec((tb, LRU_WIDTH), lambda b, t: (b * nt + t, 0)),
            pl.BlockSpec((None, SUBLANES, LRU_WIDTH), lambda b, t: (b, 0, 0)),
            pl.BlockSpec((None, SUBLANES, LRU_WIDTH), lambda b, t: (b, 0, 0)),
        ],
        out_shape=[
            jax.ShapeDtypeStruct((nb * t_len, LRU_WIDTH), BF16),
            jax.ShapeDtypeStruct((nb, SUBLANES, LRU_WIDTH), F32),
            jax.ShapeDtypeStruct((nb, SUBLANES, LRU_WIDTH), F32),
        ],
        scratch_shapes=[pltpu.VMEM((SUBLANES, LRU_WIDTH), F32), pltpu.VMEM((SUBLANES, LRU_WIDTH), F32)],
        compiler_params=_cparams(("arbitrary", "arbitrary")),
        name="lru",
    )(proj, proj, cw, cb, wai, ba, bi, lam, tail0, h0)


def _ssd_kernel(xs_ref, bc_ref, z_ref, dt_ref, cwx_ref, cbx_ref, cwb_ref, cbb_ref, dtb_ref, alog_ref,
                dexp_ref, ng_ref, tailx0_ref, tailb0_ref, s0_ref,
                y_ref, tailx_out_ref, tailb_out_ref, s_out_ref,
                tailx_sc, tailb_sc, s_sc, *, first_valid):
    c = pl.program_id(1)
    ln = SSD_CHUNK

    @pl.when(c == 0)
    def _():
        tailx_sc[...] = tailx0_ref[...]
        tailb_sc[...] = tailb0_ref[...]
        s_sc[...] = s0_ref[...]

    xr = xs_ref[...]
    br = bc_ref[...]
    xs = _silu(_causal_conv(xr, tailx_sc[...], cwx_ref[...], cbx_ref[...]))
    bc = _silu(_causal_conv(br, tailb_sc[...], cwb_ref[...], cbb_ref[...]))
    tailx_sc[...] = xr[ln - SUBLANES:ln]
    tailb_sc[...] = br[ln - SUBLANES:ln]
    dt = _softplus(dt_ref[...] + dtb_ref[...])
    if first_valid:
        keep = lax.broadcasted_iota(jnp.int32, (ln, 1), 0) >= first_valid
        xs = jnp.where(keep, xs, 0.0)
        bc = jnp.where(keep, bc, 0.0)
        dt = jnp.where(keep, dt, 0.0)

    a_neg = -jnp.exp(alog_ref[...])
    da = dt * a_neg
    rr = lax.broadcasted_iota(jnp.int32, (ln, ln), 0)
    cc = lax.broadcasted_iota(jnp.int32, (ln, ln), 1)
    causal = rr >= cc
    cs = jnp.dot(causal.astype(F32), da, preferred_element_type=F32,
                 precision=lax.Precision.HIGHEST)
    cs_last = cs[ln - 1:ln, :]
    ecs = jnp.exp(cs)
    wdec = jnp.exp(cs_last - cs) * dt
    cs_t = cs.T
    dt_t = dt.T
    wdec_t = wdec.T
    dec_col = jnp.exp(cs_t[:, ln - 1:ln])

    lane = lax.broadcasted_iota(jnp.int32, (1, LANES), 1)
    left = lane < SSD_HEAD_DIM
    xsb = xs.astype(BF16)
    y_parts = []
    for g in range(SSD_GROUPS):
        bg = bc[:, g * SSD_STATE:(g + 1) * SSD_STATE]
        cg = bc[:, (SSD_GROUPS + g) * SSD_STATE:(SSD_GROUPS + g + 1) * SSD_STATE]
        cb_mat = lax.dot_general(cg.astype(BF16), bg.astype(BF16), (((1,), (1,)), ((), ())),
                                 preferred_element_type=F32)
        bg_t = bg.T
        for jj in range(SSD_PAIRS // SSD_GROUPS):
            j = g * (SSD_PAIRS // SSD_GROUPS) + jj
            xp = xsb[:, j * LANES:(j + 1) * LANES]
            sp = s_sc[j]
            rhs = jnp.concatenate([xp, sp.astype(BF16)], axis=0)
            y_h, s_h, d_h = [], [], []
            for h in (2 * j, 2 * j + 1):
                seg = cs[:, h:h + 1] - cs_t[h:h + 1, :]
                lmat = jnp.exp(jnp.where(causal, seg, -jnp.inf)) * dt_t[h:h + 1, :]
                m_h = (cb_mat * lmat).astype(BF16)
                c_off = (cg * ecs[:, h:h + 1]).astype(BF16)
                y_h.append(jnp.dot(jnp.concatenate([m_h, c_off], axis=1), rhs,
                                   preferred_element_type=F32))
                bw_t = (bg_t * wdec_t[h:h + 1, :]).astype(BF16)
                s_h.append(jnp.dot(bw_t, xp, preferred_element_type=F32))
                d_h.append(dec_col[h:h + 1, :])
            y_parts.append(jnp.where(left, y_h[0], y_h[1]))
            s_sc[j] = sp * jnp.where(left, d_h[0], d_h[1]) + jnp.where(left, s_h[0], s_h[1])

    y = jnp.concatenate(y_parts, axis=1) + xs * dexp_ref[...]
    y = y * _silu(z_ref[...])
    gw = SSD_INNER // SSD_GROUPS
    outs = []
    for g in range(SSD_GROUPS):
        yg = y[:, g * gw:(g + 1) * gw]
        outs.append(yg * lax.rsqrt(jnp.mean(yg * yg, axis=-1, keepdims=True) + EPS))
    y_ref[...] = (jnp.concatenate(outs, axis=1) * ng_ref[...]).astype(BF16)
    tailx_out_ref[...] = xr[ln - SUBLANES:ln]
    tailb_out_ref[...] = br[ln - SUBLANES:ln]
    s_out_ref[...] = s_sc[...]


def _ssd(proj, dtraw, nb, t_len, row0_blocks, params, tailx0, tailb0, s0, first_valid):
    ln = SSD_CHUNK
    nc = t_len // ln
    per_b = (proj.shape[0] // nb) // ln
    cwx, cbx, cwb, cbb, dtb, alog, dexp, ng = params
    kern = functools.partial(_ssd_kernel, first_valid=first_valid)
    const2 = lambda b, c: (0, 0)
    rowi = lambda b, c: b * per_b + row0_blocks + c
    return pl.pallas_call(
        kern,
        grid=(nb, nc),
        in_specs=[
            pl.BlockSpec((ln, SSD_INNER), lambda b, c: (rowi(b, c), COL_XS // SSD_INNER)),
            pl.BlockSpec((ln, 1024), lambda b, c: (rowi(b, c), COL_BC // 1024)),
            pl.BlockSpec((ln, SSD_INNER), lambda b, c: (rowi(b, c), COL_Z // SSD_INNER)),
            pl.BlockSpec((ln, LANES), lambda b, c: (rowi(b, c), 0)),
            pl.BlockSpec((CONV_W, SSD_INNER), const2),
            pl.BlockSpec((1, SSD_INNER), const2),
            pl.BlockSpec((CONV_W, 1024), const2),
            pl.BlockSpec((1, 1024), const2),
            pl.BlockSpec((1, LANES), const2),
            pl.BlockSpec((1, LANES), const2),
            pl.BlockSpec((1, SSD_INNER), const2),
            pl.BlockSpec((1, SSD_INNER), const2),
            pl.BlockSpec((SUBLANES, SSD_INNER), const2),
            pl.BlockSpec((SUBLANES, 1024), const2),
            pl.BlockSpec((SSD_PAIRS, SSD_STATE, LANES), lambda b, c: (0, 0, 0)),
        ],
        out_specs=[
            pl.BlockSpec((ln, SSD_INNER), lambda b, c: (b * nc + c, 0)),
            pl.BlockSpec((None, SUBLANES, SSD_INNER), lambda b, c: (b, 0, 0)),
            pl.BlockSpec((None, SUBLANES, 1024), lambda b, c: (b, 0, 0)),
            pl.BlockSpec((None, SSD_PAIRS, SSD_STATE, LANES), lambda b, c: (b, 0, 0, 0)),
        ],
        out_shape=[
            jax.ShapeDtypeStruct((nb * t_len, SSD_INNER), BF16),
            jax.ShapeDtypeStruct((nb, SUBLANES, SSD_INNER), F32),
            jax.ShapeDtypeStruct((nb, SUBLANES, 1024), F32),
            jax.ShapeDtypeStruct((nb, SSD_PAIRS, SSD_STATE, LANES), F32),
        ],
        scratch_shapes=[
            pltpu.VMEM((SUBLANES, SSD_INNER), F32),
            pltpu.VMEM((SUBLANES, 1024), F32),
            pltpu.VMEM((SSD_PAIRS, SSD_STATE, LANES), F32),
        ],
        compiler_params=_cparams(("arbitrary", "arbitrary")),
        name="ssd",
    )(proj, proj, proj, dtraw, cwx, cbx, cwb, cbb, dtb, alog, dexp, ng, tailx0, tailb0, s0)


def _merge_kernel(a_ref, y_ref, ga_ref, gb_ref, wl_ref, ws_ref, gbias_ref, o_ref):
    oa = jnp.dot(a_ref[...], wl_ref[...], preferred_element_type=F32)
    ob = jnp.dot(y_ref[...], ws_ref[...], preferred_element_type=F32)
    gbias = gbias_ref[...]
    g_a = _sigmoid(ga_ref[...] + gbias[0:1, :])
    g_b = _sigmoid(gb_ref[...] + gbias[1:2, :])
    o_ref[...] = (g_a * oa + g_b * ob).astype(BF16)


def _merge(a_lru, y_ssd, proj, wl, ws, gbias):
    m = a_lru.shape[0]
    tm = _pick(m, (1024, 512, 256, 128))
    tn = 1024
    return pl.pallas_call(
        _merge_kernel,
        grid=(m // tm, D_MODEL // tn),
        in_specs=[
            pl.BlockSpec((tm, LRU_WIDTH), lambda i, j: (i, 0)),
            pl.BlockSpec((tm, SSD_INNER), lambda i, j: (i, 0)),
            pl.BlockSpec((tm, tn), lambda i, j: (i, COL_GA // tn + j)),
            pl.BlockSpec((tm, tn), lambda i, j: (i, COL_GB // tn + j)),
            pl.BlockSpec((LRU_WIDTH, tn), lambda i, j: (0, j)),
            pl.BlockSpec((SSD_INNER, tn), lambda i, j: (0, j)),
            pl.BlockSpec((2, tn), lambda i, j: (0, j)),
        ],
        out_specs=pl.BlockSpec((tm, tn), lambda i, j: (i, j)),
        out_shape=jax.ShapeDtypeStruct((m, D_MODEL), BF16),
        compiler_params=_cparams(("parallel", "arbitrary")),
        name="merge",
    )(a_lru, y_ssd, proj, proj, wl, ws, gbias)


def _outproj_kernel(m_ref, x_ref, wo_ref, g_ref, wr_ref, br_ref, h_ref, xn_ref, te_ref, tw_ref):
    h2 = x_ref[...] + jnp.dot(m_ref[...], wo_ref[...], preferred_element_type=F32)
    h_ref[...] = h2
    ms = jnp.mean(h2 * h2, axis=-1, keepdims=True)
    xn = h2 * lax.rsqrt(ms + EPS) * g_ref[...]
    xn_ref[...] = xn
    tm = xn.shape[0]
    x_hi = xn.astype(BF16)
    x_lo = (xn - x_hi.astype(F32)).astype(BF16)
    prod = jnp.dot(jnp.concatenate([x_hi, x_lo], axis=0), wr_ref[...], preferred_element_type=F32)
    top, bot = prod[:tm], prod[tm:]
    lane = lax.broadcasted_iota(jnp.int32, (tm, LANES), 1)
    logits = top + pltpu.roll(top, LANES - N_EXPERTS, axis=1) + bot + br_ref[...]
    vals = jnp.where(lane < N_EXPERTS, logits, -1e30)
    te = jnp.zeros((tm, LANES), jnp.int32)
    tw = jnp.zeros((tm, LANES), F32)
    m0 = None
    for k in range(TOP_K):
        mk = jnp.max(vals, axis=-1, keepdims=True)
        ik = jnp.min(jnp.where(vals == mk, lane, LANES), axis=-1, keepdims=True)
        if k == 0:
            m0 = mk
        te = jnp.where(lane == k, ik, te)
        tw = jnp.where(lane == k, jnp.exp(mk - m0), tw)
        vals = jnp.where(lane == ik, -jnp.inf, vals)
    tw = tw / jnp.sum(tw, axis=-1, keepdims=True)
    te_ref[...] = te
    tw_ref[...] = tw


def _outproj(merged, xrows, wo, g, wr, br):
    m = merged.shape[0]
    tm = _pick(m, (512, 256, 128))
    return pl.pallas_call(
        _outproj_kernel,
        grid=(m // tm,),
        in_specs=[
            pl.BlockSpec((tm, D_MODEL), lambda i: (i, 0)),
            pl.BlockSpec((tm, D_MODEL), lambda i: (i, 0)),
            pl.BlockSpec((D_MODEL, D_MODEL), lambda i: (0, 0)),
            pl.BlockSpec((1, D_MODEL), lambda i: (0, 0)),
            pl.BlockSpec((D_MODEL, LANES), lambda i: (0, 0)),
            pl.BlockSpec((1, LANES), lambda i: (0, 0)),
        ],
        out_specs=[
            pl.BlockSpec((tm, D_MODEL), lambda i: (i, 0)),
            pl.BlockSpec((tm, D_MODEL), lambda i: (i, 0)),
            pl.BlockSpec((tm, LANES), lambda i: (i, 0)),
            pl.BlockSpec((tm, LANES), lambda i: (i, 0)),
        ],
        out_shape=[
            jax.ShapeDtypeStruct((m, D_MODEL), F32),
            jax.ShapeDtypeStruct((m, D_MODEL), F32),
            jax.ShapeDtypeStruct((m, LANES), jnp.int32),
            jax.ShapeDtypeStruct((m, LANES), F32),
        ],
        compiler_params=_cparams(("parallel",)),
        name="outproj",
    )(merged, xrows, wo, g, wr, br)


def _moe_kernel(be_ref, nv_ref, idx_ref, idxn_ref, xn_hbm, wg_ref, wu_ref, wd_ref, bg_ref, bu_ref, bd_ref,
                o_ref, xbuf, xb16, sem, *, tm, nf):
    b = pl.program_id(0)
    f = pl.program_id(1)
    nv = nv_ref[0]

    slot = lax.rem(b, 2)
    groups = tm // SUBLANES

    def start_rows(idx_smem, dst_slot, g):
        for j in range(SUBLANES):
            tok = idx_smem[0, g * SUBLANES + j]
            pltpu.make_async_copy(
                xn_hbm.at[lax.shift_right_logical(tok, 3), pl.ds(lax.bitwise_and(tok, SUBLANES - 1), 1), :],
                xbuf.at[dst_slot, g, pl.ds(j, 1), :], sem.at[dst_slot]).start()

    def wait_gather(dst_slot):
        pltpu.make_async_copy(xn_hbm.at[pl.ds(0, groups)], xbuf.at[dst_slot], sem.at[dst_slot]).wait()

    @pl.when((f == 0) & (b < nv))
    def _():
        @pl.when(b == 0)
        def _():
            def body(g, carry):
                start_rows(idx_ref, 0, g)
                return carry
            lax.fori_loop(0, groups, body, 0, unroll=8)

        wait_gather(slot)
        xb16[...] = xbuf[slot].reshape(tm, D_MODEL).astype(BF16)

    @pl.when(b < nv)
    def _():
        for gi in range(groups // nf):
            start_rows(idxn_ref, 1 - slot, f * (groups // nf) + gi)

        x = xb16[...]
        gate = jnp.dot(x, wg_ref[...], preferred_element_type=F32) + bg_ref[...]
        up = jnp.dot(x, wu_ref[...], preferred_element_type=F32) + bu_ref[...]
        gate = jnp.minimum(gate, SWIGLU_LIMIT)
        up = jnp.clip(up, -SWIGLU_LIMIT, SWIGLU_LIMIT)
        act = (up + 1.0) * gate * _sigmoid(SWIGLU_ALPHA * gate)
        part = jnp.dot(act.astype(BF16), wd_ref[...], preferred_element_type=F32)

        @pl.when(f == 0)
        def _():
            o_ref[...] = part + bd_ref[...]

        @pl.when(f > 0)
        def _():
            o_ref[...] += part

        @pl.when((f == nf - 1) & (b == nv - 1))
        def _():
            wait_gather(1 - slot)

    @pl.when((b >= nv) & (f == 0))
    def _():
        o_ref[...] = jnp.zeros(o_ref.shape, F32)


def _moe(block_e, nvalid, row_tok3, xn, wgu, wd, bgu, bd, tm, tf):
    nb = row_tok3.shape[0]
    nf = D_FF // tf
    kern = functools.partial(_moe_kernel, tm=tm, nf=nf)

    def bclamp(b, nv):
        return jnp.minimum(b, nv[0] - 1)

    def fclamp(b, f, nv):
        return jnp.where(b < nv[0], f, nf - 1)

    grid_spec = pltpu.PrefetchScalarGridSpec(
        num_scalar_prefetch=2,
        grid=(nb, nf),
        in_specs=[
            pl.BlockSpec((None, 1, tm), lambda b, f, be, nv: (bclamp(b, nv), 0, 0),
                         memory_space=pltpu.SMEM),
            pl.BlockSpec((None, 1, tm), lambda b, f, be, nv: (bclamp(b + 1, nv), 0, 0),
                         memory_space=pltpu.SMEM),
            pl.BlockSpec(memory_space=pl.ANY),
            pl.BlockSpec((None, D_MODEL, tf), lambda b, f, be, nv: (be[bclamp(b, nv)], 0, fclamp(b, f, nv))),
            pl.BlockSpec((None, D_MODEL, tf),
                         lambda b, f, be, nv: (be[bclamp(b, nv)], 0, nf + fclamp(b, f, nv))),
            pl.BlockSpec((None, tf, D_MODEL), lambda b, f, be, nv: (be[bclamp(b, nv)], fclamp(b, f, nv), 0)),
            pl.BlockSpec((None, 1, tf), lambda b, f, be, nv: (be[bclamp(b, nv)], 0, fclamp(b, f, nv))),
            pl.BlockSpec((None, 1, tf), lambda b, f, be, nv: (be[bclamp(b, nv)], 0, nf + fclamp(b, f, nv))),
            pl.BlockSpec((None, 1, D_MODEL), lambda b, f, be, nv: (be[bclamp(b, nv)], 0, 0)),
        ],
        out_specs=pl.BlockSpec((tm, D_MODEL), lambda b, f, be, nv: (b, 0)),
        scratch_shapes=[
            pltpu.VMEM((2, tm // SUBLANES, SUBLANES, D_MODEL), F32),
            pltpu.VMEM((tm, D_MODEL), BF16),
            pltpu.SemaphoreType.DMA((2,)),
        ],
    )
    xn_tiles = xn.reshape(xn.shape[0] // SUBLANES, SUBLANES, D_MODEL)
    return pl.pallas_call(
        kern,
        grid_spec=grid_spec,
        out_shape=jax.ShapeDtypeStruct((nb * tm, D_MODEL), F32),
        compiler_params=_cparams(("arbitrary", "arbitrary")),
        name="moe",
    )(block_e, nvalid, row_tok3, row_tok3, xn_tiles, wgu, wgu, wd, bgu, bgu, bd)


def _combine_kernel(idx_ref, idxn_ref, y_hbm, h_ref, tw_ref, g_ref, o_ref, buf, sem, *, tc, nsteps):
    i = pl.program_id(0)
    nrow = TOP_K * tc

    def start_gather(idx_smem, slot):
        def body(r, carry):
            pltpu.make_async_copy(y_hbm.at[pl.ds(idx_smem[0, r], 1), :],
                                  buf.at[slot, pl.ds(r, 1), :], sem.at[slot]).start()
            return carry
        lax.fori_loop(0, nrow, body, 0, unroll=8)

    slot = lax.rem(i, 2)

    @pl.when(i == 0)
    def _():
        start_gather(idx_ref, 0)

    pltpu.make_async_copy(y_hbm.at[pl.ds(0, nrow), :], buf.at[slot], sem.at[slot]).wait()

    @pl.when(i + 1 < nsteps)
    def _():
        start_gather(idxn_ref, 1 - slot)

    tw = tw_ref[...]
    v = h_ref[...]
    for k in range(TOP_K):
        v = v + tw[:, k:k + 1] * buf[slot, pl.ds(k * tc, tc), :]
    ms = jnp.mean(v * v, axis=-1, keepdims=True)
    o_ref[...] = v * lax.rsqrt(ms + EPS) * g_ref[...]


def _combine(dest3, y, h2, tw, g, tc):
    m = h2.shape[0]
    nsteps = m // tc
    kern = functools.partial(_combine_kernel, tc=tc, nsteps=nsteps)
    return pl.pallas_call(
        kern,
        grid=(nsteps,),
        in_specs=[
            pl.BlockSpec((None, 1, TOP_K * tc), lambda i: (i, 0, 0), memory_space=pltpu.SMEM),
            pl.BlockSpec((None, 1, TOP_K * tc), lambda i: (jnp.minimum(i + 1, nsteps - 1), 0, 0),
                         memory_space=pltpu.SMEM),
            pl.BlockSpec(memory_space=pl.ANY),
            pl.BlockSpec((tc, D_MODEL), lambda i: (i, 0)),
            pl.BlockSpec((tc, LANES), lambda i: (i, 0)),
            pl.BlockSpec((1, D_MODEL), lambda i: (0, 0)),
        ],
        out_specs=pl.BlockSpec((tc, D_MODEL), lambda i: (i, 0)),
        out_shape=jax.ShapeDtypeStruct((m, D_MODEL), F32),
        scratch_shapes=[
            pltpu.VMEM((2, TOP_K * tc, D_MODEL), F32),
            pltpu.SemaphoreType.DMA((2,)),
        ],
        compiler_params=_cparams(("arbitrary",)),
        name="combine",
    )(dest3, dest3, y, h2, tw, g)


def _route(top_e, tm):
    m = top_e.shape[0]
    n_assign = m * TOP_K
    flat_e = top_e.reshape(-1)
    order = jnp.argsort(flat_e, stable=True).astype(jnp.int32)
    rank = jnp.argsort(order).astype(jnp.int32)
    counts = jnp.sum((flat_e[:, None] == jnp.arange(N_EXPERTS, dtype=jnp.int32)[None, :]).astype(jnp.int32),
                     axis=0)
    starts = jnp.cumsum(counts) - counts
    padded_counts = (counts + tm - 1) // tm * tm
    padded_end = jnp.cumsum(padded_counts)
    padded_start = padded_end - padded_counts
    shift = padded_start - starts
    dest = rank + shift[flat_e]
    n_blocks = n_assign // tm + N_EXPERTS
    block_row0 = jnp.arange(n_blocks, dtype=jnp.int32) * tm
    block_e = jnp.minimum(jnp.sum((block_row0[:, None] >= padded_end[None, :]).astype(jnp.int32), axis=1),
                          N_EXPERTS - 1).astype(jnp.int32)
    off = jnp.clip(block_row0 - shift[block_e], 0, n_assign)
    ctok = jnp.concatenate([order // TOP_K, jnp.zeros((tm,), jnp.int32)])
    row_tok = ctok[off[:, None] + jnp.arange(tm, dtype=jnp.int32)[None, :]]
    nvalid = (padded_end[-1] // tm).astype(jnp.int32).reshape(1)
    return row_tok.reshape(n_blocks, 1, tm), dest.reshape(m, TOP_K), block_e, nvalid


def kernel(x, meta_tokens, norm_mix_g, w_in, gate_bias, lru_conv_w, lru_conv_b, lru_wa, lru_ba, lru_wi,
           lru_bi, lru_lambda, ssd_conv_w, ssd_conv_b, ssd_dt_bias, ssd_A_log, ssd_D, ssd_norm_g,
           w_proj_lru, w_proj_ssd, w_out, norm_moe_g, w_router, b_router, w_gate_up, b_gate_up, w_down,
           b_down, norm_final_g):
    bsz, t_len, _ = x.shape
    m = bsz * t_len
    xrows = x.reshape(m, D_MODEL)

    w = w_in[0]
    dt_lo = COL_BC + 1024
    w_main = jnp.concatenate([w[:, :dt_lo], w[:, dt_lo + SSD_HEADS:]], axis=1).astype(BF16)
    w_dt = jnp.pad(w[:, dt_lo:dt_lo + SSD_HEADS], ((0, 0), (0, LANES - SSD_HEADS))).astype(BF16)
    g_mix = norm_mix_g[0].reshape(1, D_MODEL)
    wai = jnp.concatenate([lru_wa[0], lru_wi[0]], axis=-1).astype(BF16)
    lru_p = (lru_conv_w[0], lru_conv_b[0].reshape(1, -1), wai, lru_ba[0].reshape(1, -1),
             lru_bi[0].reshape(1, -1), lru_lambda[0].reshape(1, -1))
    scw, scb = ssd_conv_w[0], ssd_conv_b[0].reshape(1, -1)
    ssd_p = (scw[:, :SSD_INNER], scb[:, :SSD_INNER], scw[:, SSD_INNER:], scb[:, SSD_INNER:],
             jnp.pad(ssd_dt_bias[0], (0, LANES - SSD_HEADS)).reshape(1, LANES),
             jnp.pad(ssd_A_log[0], (0, LANES - SSD_HEADS), constant_values=-1e30).reshape(1, LANES),
             jnp.repeat(ssd_D[0], SSD_HEAD_DIM).reshape(1, SSD_INNER),
             ssd_norm_g[0].reshape(1, SSD_INNER))
    wl = w_proj_lru[0].astype(BF16)
    ws = w_proj_ssd[0].astype(BF16)
    wo = w_out[0].astype(BF16)
    wr_hi = w_router[0].astype(BF16)
    wr_lo = (w_router[0] - wr_hi.astype(F32)).astype(BF16)
    wr = jnp.pad(jnp.concatenate([wr_hi, wr_lo], axis=1), ((0, 0), (0, LANES - 2 * N_EXPERTS)))
    br = jnp.pad(b_router[0], (0, LANES - N_EXPERTS)).reshape(1, LANES)
    wgu = w_gate_up[0].astype(BF16)
    wd = w_down[0].astype(BF16)
    bgu = b_gate_up[0].reshape(N_EXPERTS, 1, 2 * D_FF)
    bd = b_down[0].reshape(N_EXPERTS, 1, D_MODEL)

    meta_rows = jnp.concatenate(
        [jnp.zeros((SSD_CHUNK - N_META, D_MODEL), F32), meta_tokens.astype(F32)], axis=0)
    proj_m, dt_m = _inproj(meta_rows, g_mix, w_main, w_dt)
    z8 = jnp.zeros((SUBLANES, LRU_WIDTH), F32)
    _, lru_tail, lru_h = _lru(proj_m, 1, N_META, (SSD_CHUNK - N_META) // N_META, N_META, *lru_p,
                              z8, z8, True)
    _, tailx, tailb, s_meta = _ssd(proj_m, dt_m, 1, SSD_CHUNK, 0, ssd_p,
                                   jnp.zeros((SUBLANES, SSD_INNER), F32),
                                   jnp.zeros((SUBLANES, 1024), F32),
                                   jnp.zeros((SSD_PAIRS, SSD_STATE, LANES), F32),
                                   SSD_CHUNK - N_META)

    proj, dtraw = _inproj(xrows, g_mix, w_main, w_dt)
    tb = _pick(t_len, (256, 128))
    a_lru, _, _ = _lru(proj, bsz, t_len, 0, tb, *lru_p, lru_tail[0], lru_h[0], False)
    y_ssd, _, _, _ = _ssd(proj, dtraw, bsz, t_len, 0, ssd_p, tailx[0], tailb[0], s_meta[0], 0)
    merged = _merge(a_lru, y_ssd, proj, wl, ws, gate_bias[0])
    h2, xn, te, tw = _outproj(merged, xrows, wo, norm_moe_g[0].reshape(1, D_MODEL), wr, br)

    tm_e = 512
    tc = 128
    row_tok3, dest, block_e, nvalid = _route(te[:, :TOP_K], tm_e)
    y_rows = _moe(block_e, nvalid, row_tok3, xn, wgu, wd, bgu, bd, tm_e, 1024)
    dest3 = dest.reshape(m // tc, tc, TOP_K).transpose(0, 2, 1).reshape(m // tc, 1, TOP_K * tc)
    out = _combine(dest3, y_rows, h2, tw, norm_final_g.reshape(1, D_MODEL), tc)
    return out.reshape(bsz, t_len, D_MODEL)
```

```python
import functools

import jax
import jax.numpy as jnp
from jax import lax
from jax.experimental import pallas as pl
from jax.experimental.pallas import tpu as pltpu

F32 = jnp.float32
BF16 = jnp.bfloat16

D_MODEL = 2048
N_META = 16
EPS = 1e-6
CONV_W = 4
LRU_WIDTH = 1024
LRU_HEADS = 8
LRU_HEAD_DIM = 128
LRU_C = 8.0
SSD_INNER = 2048
SSD_HEAD_DIM = 64
SSD_HEADS = 32
SSD_GROUPS = 4
SSD_STATE = 128
SSD_CHUNK = 128
SSD_PAIRS = SSD_HEADS // 2
N_EXPERTS = 32
TOP_K = 4
D_FF = 2048
SWIGLU_LIMIT = 7.0
SWIGLU_ALPHA = 1.702

LANES = 128
SUBLANES = 8
VMEM_LIMIT = 56 * 1024 * 1024

COL_XA, COL_YA, COL_Z, COL_XS, COL_BC, COL_GA, COL_GB = 0, 1024, 2048, 4096, 6144, 7168, 9216
PROJ_W = 11264


def _cparams(sem):
    return pltpu.CompilerParams(dimension_semantics=sem, vmem_limit_bytes=VMEM_LIMIT)


def _sigmoid(v):
    return 1.0 / (1.0 + jnp.exp(-v))


def _silu(v):
    return v * _sigmoid(v)


def _softplus(v):
    return jnp.maximum(v, 0.0) + jnp.log1p(jnp.exp(-jnp.abs(v)))


def _gelu_tanh(v):
    return 0.5 * v * (1.0 + jnp.tanh(0.7978845608028654 * (v + 0.044715 * v * v * v)))


def _pick(n, candidates):
    for c in candidates:
        if n % c == 0:
            return c
    raise ValueError(f"no tile for {n}")


def _inproj_kernel(x_ref, g_ref, w_ref, wdt_ref, o_ref, odt_ref, xn_ref):
    @pl.when(pl.program_id(1) == 0)
    def _():
        x = x_ref[...]
        ms = jnp.mean(x * x, axis=-1, keepdims=True)
        xn = (x * lax.rsqrt(ms + EPS) * g_ref[...]).astype(BF16)
        xn_ref[...] = xn
        odt_ref[...] = jnp.dot(xn, wdt_ref[...], preferred_element_type=F32)

    o_ref[...] = jnp.dot(xn_ref[...], w_ref[...], preferred_element_type=F32)


def _inproj(xr, g, w_main, w_dt):
    rows = xr.shape[0]
    tm = _pick(rows, (1024, 512, 256, 128))
    tn = 1024
    return pl.pallas_call(
        _inproj_kernel,
        grid=(rows // tm, PROJ_W // tn),
        in_specs=[
            pl.BlockSpec((tm, D_MODEL), lambda i, j: (i, 0)),
            pl.BlockSpec((1, D_MODEL), lambda i, j: (0, 0)),
            pl.BlockSpec((D_MODEL, tn), lambda i, j: (0, j)),
            pl.BlockSpec((D_MODEL, LANES), lambda i, j: (0, 0)),
        ],
        out_specs=[
            pl.BlockSpec((tm, tn), lambda i, j: (i, j)),
            pl.BlockSpec((tm, LANES), lambda i, j: (i, 0)),
        ],
        out_shape=[
            jax.ShapeDtypeStruct((rows, PROJ_W), F32),
            jax.ShapeDtypeStruct((rows, LANES), F32),
        ],
        scratch_shapes=[pltpu.VMEM((tm, D_MODEL), BF16)],
        compiler_params=_cparams(("parallel", "arbitrary")),
        name="inproj",
    )(xr, g, w_main, w_dt)


def _causal_conv(x, tail, cw, cb):
    n = x.shape[0]
    ext = jnp.concatenate([tail, x], axis=0)
    y = cb + cw[CONV_W - 1:CONV_W] * x
    for k in range(1, CONV_W):
        y = y + cw[CONV_W - 1 - k:CONV_W - k] * pltpu.roll(ext, k, axis=0)[SUBLANES:SUBLANES + n]
    return y


def _lru_kernel(xa_ref, ya_ref, cw_ref, cb_ref, wai_ref, ba_ref, bi_ref, lam_ref, tail0_ref, h0_ref,
                o_ref, tail_out_ref, h_out_ref, tail_sc, h_sc, *, tb, seq_start):
    t = pl.program_id(1)

    @pl.when(t == 0)
    def _():
        tail_sc[...] = tail0_ref[...]
        h_sc[...] = h0_ref[...]

    xa = xa_ref[...]
    xc = _causal_conv(xa, tail_sc[...], cw_ref[...], cb_ref[...])
    tail_sc[...] = xa[tb - SUBLANES:tb]

    xcb = xc.astype(BF16)
    r_parts, i_parts = [], []
    for h in range(LRU_HEADS):
        g = jnp.dot(xcb[:, h * LRU_HEAD_DIM:(h + 1) * LRU_HEAD_DIM], wai_ref[h],
                    preferred_element_type=F32)
        r_parts.append(g[:, :LRU_HEAD_DIM])
        i_parts.append(g[:, LRU_HEAD_DIM:])
    r = _sigmoid(jnp.concatenate(r_parts, axis=1) + ba_ref[...])
    ig = _sigmoid(jnp.concatenate(i_parts, axis=1) + bi_ref[...])

    log_a = LRU_C * r * (-_softplus(-lam_ref[...]))
    a = jnp.exp(log_a)
    mult = jnp.sqrt(-jnp.tanh(log_a) * (a * a + 1.0))
    row = lax.broadcasted_iota(jnp.int32, (tb, LRU_WIDTH), 0)
    if seq_start:
        mult = jnp.where((row == 0) & (t == 0), 1.0, mult)
    u = mult * (ig * xc)

    s = 1
    while s < tb:
        a_sh = pltpu.roll(a, s, axis=0)
        u_sh = pltpu.roll(u, s, axis=0)
        valid = row >= s
        u = jnp.where(valid, a * u_sh, 0.0) + u
        a = jnp.where(valid, a * a_sh, a)
        s *= 2
    h = a * h_sc[0:1, :] + u
    h_last = jnp.broadcast_to(h[tb - 1:tb, :], (SUBLANES, LRU_WIDTH))
    h_sc[...] = h_last
    o_ref[...] = (_gelu_tanh(ya_ref[...]) * h).astype(BF16)
    tail_out_ref[...] = xa[tb - SUBLANES:tb]
    h_out_ref[...] = h_last


def _lru(proj, nb, t_len, row0_blocks, tb, cw, cb, wai, ba, bi, lam, tail0, h0, seq_start):
    nt = t_len // tb
    kern = functools.partial(_lru_kernel, tb=tb, seq_start=seq_start)
    per_b = (proj.shape[0] // nb) // tb
    return pl.pallas_call(
        kern,
        grid=(nb, nt),
        in_specs=[
            pl.BlockSpec((tb, LRU_WIDTH), lambda b, t: (b * per_b + row0_blocks + t, COL_XA // LRU_WIDTH)),
            pl.BlockSpec((tb, LRU_WIDTH), lambda b, t: (b * per_b + row0_blocks + t, COL_YA // LRU_WIDTH)),
            pl.BlockSpec((CONV_W, LRU_WIDTH), lambda b, t: (0, 0)),
            pl.BlockSpec((1, LRU_WIDTH), lambda b, t: (0, 0)),
            pl.BlockSpec((LRU_HEADS, LRU_HEAD_DIM, 2 * LRU_HEAD_DIM), lambda b, t: (0, 0, 0)),
            pl.BlockSpec((1, LRU_WIDTH), lambda b, t: (0, 0)),
            pl.BlockSpec((1, LRU_WIDTH), lambda b, t: (0, 0)),
            pl.BlockSpec((1, LRU_WIDTH), lambda b, t: (0, 0)),
            pl.BlockSpec((SUBLANES, LRU_WIDTH), lambda b, t: (0, 0)),
            pl.BlockSpec((SUBLANES, LRU_WIDTH), lambda b, t: (0, 0)),
        ],
        out_specs=[
            pl.BlockSpec((tb, LRU_WIDTH), lambda b, t: (b * nt + t, 0)),
            pl.BlockSpec((None, SUBLANES, LRU_WIDTH), lambda b, t: (b, 0, 0)),
            pl.BlockSpec((None, SUBLANES, LRU_WIDTH), lambda b, t: (b, 0, 0)),
        ],
        out_shape=[
            jax.ShapeDtypeStruct((nb * t_len, LRU_WIDTH), BF16),
            jax.ShapeDtypeStruct((nb, SUBLANES, LRU_WIDTH), F32),
            jax.ShapeDtypeStruct((nb, SUBLANES, LRU_WIDTH), F32),
        ],
        scratch_shapes=[pltpu.VMEM((SUBLANES, LRU_WIDTH), F32), pltpu.VMEM((SUBLANES, LRU_WIDTH), F32)],
        compiler_params=_cparams(("arbitrary", "arbitrary")),
        name="lru",
    )(proj, proj, cw, cb, wai, ba, bi, lam, tail0, h0)


def _ssd_kernel(xs_ref, bc_ref, z_ref, dt_ref, cwx_ref, cbx_ref, cwb_ref, cbb_ref, dtb_ref, alog_ref,
                dexp_ref, ng_ref, tailx0_ref, tailb0_ref, s0_ref,
                y_ref, tailx_out_ref, tailb_out_ref, s_out_ref,
                tailx_sc, tailb_sc, s_sc, *, first_valid):
    c = pl.program_id(1)
    ln = SSD_CHUNK

    @pl.when(c == 0)
    def _():
        tailx_sc[...] = tailx0_ref[...]
        tailb_sc[...] = tailb0_ref[...]
        s_sc[...] = s0_ref[...]

    xr = xs_ref[...]
    br = bc_ref[...]
    xs = _silu(_causal_conv(xr, tailx_sc[...], cwx_ref[...], cbx_ref[...]))
    bc = _silu(_causal_conv(br, tailb_sc[...], cwb_ref[...], cbb_ref[...]))
    tailx_sc[...] = xr[ln - SUBLANES:ln]
    tailb_sc[...] = br[ln - SUBLANES:ln]
    dt = _softplus(dt_ref[...] + dtb_ref[...])
    if first_valid:
        keep = lax.broadcasted_iota(jnp.int32, (ln, 1), 0) >= first_valid
        xs = jnp.where(keep, xs, 0.0)
        bc = jnp.where(keep, bc, 0.0)
        dt = jnp.where(keep, dt, 0.0)

    a_neg = -jnp.exp(alog_ref[...])
    da = dt * a_neg
    rr = lax.broadcasted_iota(jnp.int32, (ln, ln), 0)
    cc = lax.broadcasted_iota(jnp.int32, (ln, ln), 1)
    causal = rr >= cc
    cs = jnp.dot(causal.astype(F32), da, preferred_element_type=F32,
                 precision=lax.Precision.HIGHEST)
    cs_last = cs[ln - 1:ln, :]
    ecs = jnp.exp(cs)
    wdec = jnp.exp(cs_last - cs) * dt
    cs_t = cs.T
    dt_t = dt.T
    wdec_t = wdec.T
    dec_col = jnp.exp(cs_t[:, ln - 1:ln])

    lane = lax.broadcasted_iota(jnp.int32, (1, LANES), 1)
    left = lane < SSD_HEAD_DIM
    xsb = xs.astype(BF16)
    y_parts = []
    for g in range(SSD_GROUPS):
        bg = bc[:, g * SSD_STATE:(g + 1) * SSD_STATE]
        cg = bc[:, (SSD_GROUPS + g) * SSD_STATE:(SSD_GROUPS + g + 1) * SSD_STATE]
        cb_mat = lax.dot_general(cg.astype(BF16), bg.astype(BF16), (((1,), (1,)), ((), ())),
                                 preferred_element_type=F32)
        bg_t = bg.T
        for jj in range(SSD_PAIRS // SSD_GROUPS):
            j = g * (SSD_PAIRS // SSD_GROUPS) + jj
            xp = xsb[:, j * LANES:(j + 1) * LANES]
            sp = s_sc[j]
            rhs = jnp.concatenate([xp, sp.astype(BF16)], axis=0)
            y_h, s_h, d_h = [], [], []
            for h in (2 * j, 2 * j + 1):
                seg = cs[:, h:h + 1] - cs_t[h:h + 1, :]
                lmat = jnp.exp(jnp.where(causal, seg, -jnp.inf)) * dt_t[h:h + 1, :]
                m_h = (cb_mat * lmat).astype(BF16)
                c_off = (cg * ecs[:, h:h + 1]).astype(BF16)
                y_h.append(jnp.dot(jnp.concatenate([m_h, c_off], axis=1), rhs,
                                   preferred_element_type=F32))
                bw_t = (bg_t * wdec_t[h:h + 1, :]).astype(BF16)
                s_h.append(jnp.dot(bw_t, xp, preferred_element_type=F32))
                d_h.append(dec_col[h:h + 1, :])
            y_parts.append(jnp.where(left, y_h[0], y_h[1]))
            s_sc[j] = sp * jnp.where(left, d_h[0], d_h[1]) + jnp.where(left, s_h[0], s_h[1])

    y = jnp.concatenate(y_parts, axis=1) + xs * dexp_ref[...]
    y = y * _silu(z_ref[...])
    gw = SSD_INNER // SSD_GROUPS
    outs = []
    for g in range(SSD_GROUPS):
        yg = y[:, g * gw:(g + 1) * gw]
        outs.append(yg * lax.rsqrt(jnp.mean(yg * yg, axis=-1, keepdims=True) + EPS))
    y_ref[...] = (jnp.concatenate(outs, axis=1) * ng_ref[...]).astype(BF16)
    tailx_out_ref[...] = xr[ln - SUBLANES:ln]
    tailb_out_ref[...] = br[ln - SUBLANES:ln]
    s_out_ref[...] = s_sc[...]


def _ssd(proj, dtraw, nb, t_len, row0_blocks, params, tailx0, tailb0, s0, first_valid):
    ln = SSD_CHUNK
    nc = t_len // ln
    per_b = (proj.shape[0] // nb) // ln
    cwx, cbx, cwb, cbb, dtb, alog, dexp, ng = params
    kern = functools.partial(_ssd_kernel, first_valid=first_valid)
    const2 = lambda b, c: (0, 0)
    rowi = lambda b, c: b * per_b + row0_blocks + c
    return pl.pallas_call(
        kern,
        grid=(nb, nc),
        in_specs=[
            pl.BlockSpec((ln, SSD_INNER), lambda b, c: (rowi(b, c), COL_XS // SSD_INNER)),
            pl.BlockSpec((ln, 1024), lambda b, c: (rowi(b, c), COL_BC // 1024)),
            pl.BlockSpec((ln, SSD_INNER), lambda b, c: (rowi(b, c), COL_Z // SSD_INNER)),
            pl.BlockSpec((ln, LANES), lambda b, c: (rowi(b, c), 0)),
            pl.BlockSpec((CONV_W, SSD_INNER), const2),
            pl.BlockSpec((1, SSD_INNER), const2),
            pl.BlockSpec((CONV_W, 1024), const2),
            pl.BlockSpec((1, 1024), const2),
            pl.BlockSpec((1, LANES), const2),
            pl.BlockSpec((1, LANES), const2),
            pl.BlockSpec((1, SSD_INNER), const2),
            pl.BlockSpec((1, SSD_INNER), const2),
            pl.BlockSpec((SUBLANES, SSD_INNER), const2),
            pl.BlockSpec((SUBLANES, 1024), const2),
            pl.BlockSpec((SSD_PAIRS, SSD_STATE, LANES), lambda b, c: (0, 0, 0)),
        ],
        out_specs=[
            pl.BlockSpec((ln, SSD_INNER), lambda b, c: (b * nc + c, 0)),
            pl.BlockSpec((None, SUBLANES, SSD_INNER), lambda b, c: (b, 0, 0)),
            pl.BlockSpec((None, SUBLANES, 1024), lambda b, c: (b, 0, 0)),
            pl.BlockSpec((None, SSD_PAIRS, SSD_STATE, LANES), lambda b, c: (b, 0, 0, 0)),
        ],
        out_shape=[
            jax.ShapeDtypeStruct((nb * t_len, SSD_INNER), BF16),
            jax.ShapeDtypeStruct((nb, SUBLANES, SSD_INNER), F32),
            jax.ShapeDtypeStruct((nb, SUBLANES, 1024), F32),
            jax.ShapeDtypeStruct((nb, SSD_PAIRS, SSD_STATE, LANES), F32),
        ],
        scratch_shapes=[
            pltpu.VMEM((SUBLANES, SSD_INNER), F32),
            pltpu.VMEM((SUBLANES, 1024), F32),
            pltpu.VMEM((SSD_PAIRS, SSD_STATE, LANES), F32),
        ],
        compiler_params=_cparams(("arbitrary", "arbitrary")),
        name="ssd",
    )(proj, proj, proj, dtraw, cwx, cbx, cwb, cbb, dtb, alog, dexp, ng, tailx0, tailb0, s0)


def _merge_kernel(a_ref, y_ref, ga_ref, gb_ref, wl_ref, ws_ref, gbias_ref, o_ref):
    oa = jnp.dot(a_ref[...], wl_ref[...], preferred_element_type=F32)
    ob = jnp.dot(y_ref[...], ws_ref[...], preferred_element_type=F32)
    gbias = gbias_ref[...]
    g_a = _sigmoid(ga_ref[...] + gbias[0:1, :])
    g_b = _sigmoid(gb_ref[...] + gbias[1:2, :])
    o_ref[...] = (g_a * oa + g_b * ob).astype(BF16)


def _merge(a_lru, y_ssd, proj, wl, ws, gbias):
    m = a_lru.shape[0]
    tm = _pick(m, (1024, 512, 256, 128))
    tn = 1024
    return pl.pallas_call(
        _merge_kernel,
        grid=(m // tm, D_MODEL // tn),
        in_specs=[
            pl.BlockSpec((tm, LRU_WIDTH), lambda i, j: (i, 0)),
            pl.BlockSpec((tm, SSD_INNER), lambda i, j: (i, 0)),
            pl.BlockSpec((tm, tn), lambda i, j: (i, COL_GA // tn + j)),
            pl.BlockSpec((tm, tn), lambda i, j: (i, COL_GB // tn + j)),
            pl.BlockSpec((LRU_WIDTH, tn), lambda i, j: (0, j)),
            pl.BlockSpec((SSD_INNER, tn), lambda i, j: (0, j)),
            pl.BlockSpec((2, tn), lambda i, j: (0, j)),
        ],
        out_specs=pl.BlockSpec((tm, tn), lambda i, j: (i, j)),
        out_shape=jax.ShapeDtypeStruct((m, D_MODEL), BF16),
        compiler_params=_cparams(("parallel", "arbitrary")),
        name="merge",
    )(a_lru, y_ssd, proj, proj, wl, ws, gbias)


def _outproj_kernel(m_ref, x_ref, wo_ref, g_ref, wr_ref, br_ref, h_ref, xn_ref, te_ref, tw_ref):
    h2 = x_ref[...] + jnp.dot(m_ref[...], wo_ref[...], preferred_element_type=F32)
    h_ref[...] = h2
    ms = jnp.mean(h2 * h2, axis=-1, keepdims=True)
    xn = h2 * lax.rsqrt(ms + EPS) * g_ref[...]
    xn_ref[...] = xn
    tm = xn.shape[0]
    x_hi = xn.astype(BF16)
    x_lo = (xn - x_hi.astype(F32)).astype(BF16)
    prod = jnp.dot(jnp.concatenate([x_hi, x_lo], axis=0), wr_ref[...], preferred_element_type=F32)
    top, bot = prod[:tm], prod[tm:]
    lane = lax.broadcasted_iota(jnp.int32, (tm, LANES), 1)
    logits = top + pltpu.roll(top, LANES - N_EXPERTS, axis=1) + bot + br_ref[...]
    vals = jnp.where(lane < N_EXPERTS, logits, -1e30)
    te = jnp.zeros((tm, LANES), jnp.int32)
    tw = jnp.zeros((tm, LANES), F32)
    m0 = None
    for k in range(TOP_K):
        mk = jnp.max(vals, axis=-1, keepdims=True)
        ik = jnp.min(jnp.where(vals == mk, lane, LANES), axis=-1, keepdims=True)
        if k == 0:
            m0 = mk
        te = jnp.where(lane == k, ik, te)
        tw = jnp.where(lane == k, jnp.exp(mk - m0), tw)
        vals = jnp.where(lane == ik, -jnp.inf, vals)
    tw = tw / jnp.sum(tw, axis=-1, keepdims=True)
    te_ref[...] = te
    tw_ref[...] = tw


def _outproj(merged, xrows, wo, g, wr, br):
    m = merged.shape[0]
    tm = _pick(m, (512, 256, 128))
    return pl.pallas_call(
        _outproj_kernel,
        grid=(m // tm,),
        in_specs=[
            pl.BlockSpec((tm, D_MODEL), lambda i: (i, 0)),
            pl.BlockSpec((tm, D_MODEL), lambda i: (i, 0)),
            pl.BlockSpec((D_MODEL, D_MODEL), lambda i: (0, 0)),
            pl.BlockSpec((1, D_MODEL), lambda i: (0, 0)),
            pl.BlockSpec((D_MODEL, LANES), lambda i: (0, 0)),
            pl.BlockSpec((1, LANES), lambda i: (0, 0)),
        ],
        out_specs=[
            pl.BlockSpec((tm, D_MODEL), lambda i: (i, 0)),
            pl.BlockSpec((tm, D_MODEL), lambda i: (i, 0)),
            pl.BlockSpec((tm, LANES), lambda i: (i, 0)),
            pl.BlockSpec((tm, LANES), lambda i: (i, 0)),
        ],
        out_shape=[
            jax.ShapeDtypeStruct((m, D_MODEL), F32),
            jax.ShapeDtypeStruct((m, D_MODEL), F32),
            jax.ShapeDtypeStruct((m, LANES), jnp.int32),
            jax.ShapeDtypeStruct((m, LANES), F32),
        ],
        compiler_params=_cparams(("parallel",)),
        name="outproj",
    )(merged, xrows, wo, g, wr, br)


def _moe_kernel(be_ref, nh_ref, nv_ref, idx_ref, idxn_ref, xn_hbm, wg_ref, wu_ref, wd_ref, bg_ref, bu_ref,
                bd_ref, o_ref, xbuf, xb16, sem, *, th, nf):
    b = pl.program_id(0)
    f = pl.program_id(1)
    nv = nv_ref[0]
    groups = 2 * th // SUBLANES

    def start_rows(idx_smem, g):
        for j in range(SUBLANES):
            tok = idx_smem[0, g * SUBLANES + j]
            pltpu.make_async_copy(
                xn_hbm.at[lax.shift_right_logical(tok, 3), pl.ds(lax.bitwise_and(tok, SUBLANES - 1), 1), :],
                xbuf.at[g, pl.ds(j, 1), :], sem.at[0]).start()

    def wait_gather():
        pltpu.make_async_copy(xn_hbm.at[pl.ds(0, groups)], xbuf, sem.at[0]).wait()

    @pl.when((f == 0) & (b < nv))
    def _():
        @pl.when(b == 0)
        def _():
            def body(g, carry):
                start_rows(idx_ref, g)
                return carry
            lax.fori_loop(0, groups, body, 0, unroll=8)

        wait_gather()
        xb16[...] = xbuf[...].reshape(2 * th, D_MODEL).astype(BF16)
        o_ref[...] = jnp.broadcast_to(bd_ref[...], o_ref.shape)

    def half(r0):
        x = xb16[r0:r0 + th, :]
        gate = jnp.dot(x, wg_ref[...].astype(BF16), preferred_element_type=F32) + bg_ref[...]
        up = jnp.dot(x, wu_ref[...].astype(BF16), preferred_element_type=F32) + bu_ref[...]
        gate = jnp.minimum(gate, SWIGLU_LIMIT)
        up = jnp.clip(up, -SWIGLU_LIMIT, SWIGLU_LIMIT)
        act = (up + 1.0) * gate * _sigmoid(SWIGLU_ALPHA * gate)
        o_ref[r0:r0 + th, :] += jnp.dot(act.astype(BF16), wd_ref[...].astype(BF16),
                                        preferred_element_type=F32)

    @pl.when(b < nv)
    def _():
        for gi in range(groups // nf):
            start_rows(idxn_ref, f * (groups // nf) + gi)
        half(0)

    @pl.when((b < nv) & (nh_ref[b] == 2))
    def _():
        half(th)

    @pl.when((f == nf - 1) & (b == nv - 1))
    def _():
        wait_gather()

    @pl.when((b >= nv) & (f == 0))
    def _():
        o_ref[...] = jnp.zeros(o_ref.shape, F32)


def _moe(block_e, nhalf, nvalid, row_tok3, xn, wgu, wd, bgu, bd, th, tf):
    nb = row_tok3.shape[0]
    tm = 2 * th
    nf = D_FF // tf
    kern = functools.partial(_moe_kernel, th=th, nf=nf)

    def bclamp(b, nv):
        return jnp.minimum(b, nv[0] - 1)

    def fclamp(b, f, nv):
        return jnp.where(b < nv[0], f, nf - 1)

    def wspec(shape, col):
        return pl.BlockSpec(shape, lambda b, f, be, nh, nv: (be[bclamp(b, nv)],) + col(fclamp(b, f, nv)))

    grid_spec = pltpu.PrefetchScalarGridSpec(
        num_scalar_prefetch=3,
        grid=(nb, nf),
        in_specs=[
            pl.BlockSpec((None, 1, tm), lambda b, f, be, nh, nv: (bclamp(b, nv), 0, 0),
                         memory_space=pltpu.SMEM),
            pl.BlockSpec((None, 1, tm), lambda b, f, be, nh, nv: (bclamp(b + 1, nv), 0, 0),
                         memory_space=pltpu.SMEM),
            pl.BlockSpec(memory_space=pl.ANY),
            wspec((None, D_MODEL, tf), lambda fc: (0, fc)),
            wspec((None, D_MODEL, tf), lambda fc: (0, nf + fc)),
            wspec((None, tf, D_MODEL), lambda fc: (fc, 0)),
            wspec((None, 1, tf), lambda fc: (0, fc)),
            wspec((None, 1, tf), lambda fc: (0, nf + fc)),
            wspec((None, 1, D_MODEL), lambda fc: (0, 0)),
        ],
        out_specs=pl.BlockSpec((tm, D_MODEL), lambda b, f, be, nh, nv: (b, 0)),
        scratch_shapes=[
            pltpu.VMEM((tm // SUBLANES, SUBLANES, D_MODEL), F32),
            pltpu.VMEM((tm, D_MODEL), BF16),
            pltpu.SemaphoreType.DMA((1,)),
        ],
    )
    xn_tiles = xn.reshape(xn.shape[0] // SUBLANES, SUBLANES, D_MODEL)
    return pl.pallas_call(
        kern,
        grid_spec=grid_spec,
        out_shape=jax.ShapeDtypeStruct((nb * tm, D_MODEL), F32),
        compiler_params=_cparams(("arbitrary", "arbitrary")),
        name="moe",
    )(block_e, nhalf, nvalid, row_tok3, row_tok3, xn_tiles, wgu, wgu, wd, bgu, bgu, bd)


def _combine_kernel(idx_ref, idxn_ref, y_hbm, h_ref, tw_ref, g_ref, o_ref, buf, sem, *, tc, nsteps):
    i = pl.program_id(0)
    nrow = TOP_K * tc

    def start_gather(idx_smem, slot):
        def body(r, carry):
            pltpu.make_async_copy(y_hbm.at[pl.ds(idx_smem[0, r], 1), :],
                                  buf.at[slot, pl.ds(r, 1), :], sem.at[slot]).start()
            return carry
        lax.fori_loop(0, nrow, body, 0, unroll=8)

    slot = lax.rem(i, 2)

    @pl.when(i == 0)
    def _():
        start_gather(idx_ref, 0)

    pltpu.make_async_copy(y_hbm.at[pl.ds(0, nrow), :], buf.at[slot], sem.at[slot]).wait()

    @pl.when(i + 1 < nsteps)
    def _():
        start_gather(idxn_ref, 1 - slot)

    tw = tw_ref[...]
    v = h_ref[...]
    for k in range(TOP_K):
        v = v + tw[:, k:k + 1] * buf[slot, pl.ds(k * tc, tc), :]
    ms = jnp.mean(v * v, axis=-1, keepdims=True)
    o_ref[...] = v * lax.rsqrt(ms + EPS) * g_ref[...]


def _combine(dest3, y, h2, tw, g, tc):
    m = h2.shape[0]
    nsteps = m // tc
    kern = functools.partial(_combine_kernel, tc=tc, nsteps=nsteps)
    return pl.pallas_call(
        kern,
        grid=(nsteps,),
        in_specs=[
            pl.BlockSpec((None, 1, TOP_K * tc), lambda i: (i, 0, 0), memory_space=pltpu.SMEM),
            pl.BlockSpec((None, 1, TOP_K * tc), lambda i: (jnp.minimum(i + 1, nsteps - 1), 0, 0),
                         memory_space=pltpu.SMEM),
            pl.BlockSpec(memory_space=pl.ANY),
            pl.BlockSpec((tc, D_MODEL), lambda i: (i, 0)),
            pl.BlockSpec((tc, LANES), lambda i: (i, 0)),
            pl.BlockSpec((1, D_MODEL), lambda i: (0, 0)),
        ],
        out_specs=pl.BlockSpec((tc, D_MODEL), lambda i: (i, 0)),
        out_shape=jax.ShapeDtypeStruct((m, D_MODEL), F32),
        scratch_shapes=[
            pltpu.VMEM((2, TOP_K * tc, D_MODEL), F32),
            pltpu.SemaphoreType.DMA((2,)),
        ],
        compiler_params=_cparams(("arbitrary",)),
        name="combine",
    )(dest3, dest3, y, h2, tw, g)


def _route(top_e, tm, th):
    m = top_e.shape[0]
    n_assign = m * TOP_K
    flat_e = top_e.reshape(-1)
    order = jnp.argsort(flat_e, stable=True).astype(jnp.int32)
    rank = jnp.argsort(order).astype(jnp.int32)
    counts = jnp.sum((flat_e[:, None] == jnp.arange(N_EXPERTS, dtype=jnp.int32)[None, :]).astype(jnp.int32),
                     axis=0)
    starts = jnp.cumsum(counts) - counts
    padded_counts = (counts + tm - 1) // tm * tm
    padded_end = jnp.cumsum(padded_counts)
    padded_start = padded_end - padded_counts
    shift = padded_start - starts
    dest = rank + shift[flat_e]
    n_blocks = n_assign // tm + N_EXPERTS
    block_row0 = jnp.arange(n_blocks, dtype=jnp.int32) * tm
    block_e = jnp.minimum(jnp.sum((block_row0[:, None] >= padded_end[None, :]).astype(jnp.int32), axis=1),
                          N_EXPERTS - 1).astype(jnp.int32)
    off = jnp.clip(block_row0 - shift[block_e], 0, n_assign)
    ctok = jnp.concatenate([order // TOP_K, jnp.zeros((tm,), jnp.int32)])
    row_tok = ctok[off[:, None] + jnp.arange(tm, dtype=jnp.int32)[None, :]]
    nvalid = (padded_end[-1] // tm).astype(jnp.int32).reshape(1)
    rows_in_block = jnp.clip(counts[block_e] - (block_row0 - padded_start[block_e]), 0, tm)
    nhalf = jnp.where(block_row0 < padded_end[-1], (rows_in_block + th - 1) // th, 0).astype(jnp.int32)
    return row_tok.reshape(n_blocks, 1, tm), dest.reshape(m, TOP_K), block_e, nhalf, nvalid


def kernel(x, meta_tokens, norm_mix_g, w_in, gate_bias, lru_conv_w, lru_conv_b, lru_wa, lru_ba, lru_wi,
           lru_bi, lru_lambda, ssd_conv_w, ssd_conv_b, ssd_dt_bias, ssd_A_log, ssd_D, ssd_norm_g,
           w_proj_lru, w_proj_ssd, w_out, norm_moe_g, w_router, b_router, w_gate_up, b_gate_up, w_down,
           b_down, norm_final_g):
    bsz, t_len, _ = x.shape
    m = bsz * t_len
    xrows = x.reshape(m, D_MODEL)

    w = w_in[0]
    dt_lo = COL_BC + 1024
    w_main = jnp.concatenate([w[:, :dt_lo], w[:, dt_lo + SSD_HEADS:]], axis=1).astype(BF16)
    w_dt = jnp.pad(w[:, dt_lo:dt_lo + SSD_HEADS], ((0, 0), (0, LANES - SSD_HEADS))).astype(BF16)
    g_mix = norm_mix_g[0].reshape(1, D_MODEL)
    wai = jnp.concatenate([lru_wa[0], lru_wi[0]], axis=-1).astype(BF16)
    lru_p = (lru_conv_w[0], lru_conv_b[0].reshape(1, -1), wai, lru_ba[0].reshape(1, -1),
             lru_bi[0].reshape(1, -1), lru_lambda[0].reshape(1, -1))
    scw, scb = ssd_conv_w[0], ssd_conv_b[0].reshape(1, -1)
    ssd_p = (scw[:, :SSD_INNER], scb[:, :SSD_INNER], scw[:, SSD_INNER:], scb[:, SSD_INNER:],
             jnp.pad(ssd_dt_bias[0], (0, LANES - SSD_HEADS)).reshape(1, LANES),
             jnp.pad(ssd_A_log[0], (0, LANES - SSD_HEADS), constant_values=-1e30).reshape(1, LANES),
             jnp.repeat(ssd_D[0], SSD_HEAD_DIM).reshape(1, SSD_INNER),
             ssd_norm_g[0].reshape(1, SSD_INNER))
    wl = w_proj_lru[0].astype(BF16)
    ws = w_proj_ssd[0].astype(BF16)
    wo = w_out[0].astype(BF16)
    wr_hi = w_router[0].astype(BF16)
    wr_lo = (w_router[0] - wr_hi.astype(F32)).astype(BF16)
    wr = jnp.pad(jnp.concatenate([wr_hi, wr_lo], axis=1), ((0, 0), (0, LANES - 2 * N_EXPERTS)))
    br = jnp.pad(b_router[0], (0, LANES - N_EXPERTS)).reshape(1, LANES)
    bgu =b_gate_up[0].reshape(N_EXPERTS, 1, 2 * D_FF)
    bd = b_down[0].reshape(N_EXPERTS, 1, D_MODEL)

    meta_rows = jnp.concatenate(
        [jnp.zeros((SSD_CHUNK - N_META, D_MODEL), F32), meta_tokens.astype(F32)], axis=0)
    proj_m, dt_m = _inproj(meta_rows, g_mix, w_main, w_dt)
    z8 = jnp.zeros((SUBLANES, LRU_WIDTH), F32)
    _, lru_tail, lru_h = _lru(proj_m, 1, N_META, (SSD_CHUNK - N_META) // N_META, N_META, *lru_p,
                              z8, z8, True)
    _, tailx, tailb, s_meta = _ssd(proj_m, dt_m, 1, SSD_CHUNK, 0, ssd_p,
                                   jnp.zeros((SUBLANES, SSD_INNER), F32),
                                   jnp.zeros((SUBLANES, 1024), F32),
                                   jnp.zeros((SSD_PAIRS, SSD_STATE, LANES), F32),
                                   SSD_CHUNK - N_META)

    proj, dtraw = _inproj(xrows, g_mix, w_main, w_dt)
    tb = _pick(t_len, (256, 128))
    a_lru, _, _ = _lru(proj, bsz, t_len, 0, tb, *lru_p, lru_tail[0], lru_h[0], False)
    y_ssd, _, _, _ = _ssd(proj, dtraw, bsz, t_len, 0, ssd_p, tailx[0], tailb[0], s_meta[0], 0)
    merged = _merge(a_lru, y_ssd, proj, wl, ws, gate_bias[0])
    h2, xn, te, tw = _outproj(merged, xrows, wo, norm_moe_g[0].reshape(1, D_MODEL), wr, br)

    th_e = 512
    tc = 128
    row_tok3, dest, block_e, nhalf, nvalid = _route(te[:, :TOP_K], 2 * th_e, th_e)
    y_rows = _moe(block_e, nhalf, nvalid, row_tok3, xn, w_gate_up[0], w_down[0], bgu, bd, th_e, 512)
    dest3 = dest.reshape(m // tc, tc, TOP_K).transpose(0, 2, 1).reshape(m // tc, 1, TOP_K * tc)
    out = _combine(dest3, y_rows, h2, tw, norm_final_g.reshape(1, D_MODEL), tc)
    return out.reshape(bsz, t_len, D_MODEL)
```

```python
import functools

import jax
import jax.numpy as jnp
from jax import lax
from jax.experimental import pallas as pl
from jax.experimental.pallas import tpu as pltpu

F32 = jnp.float32
BF16 = jnp.bfloat16

D_MODEL = 2048
N_META = 16
EPS = 1e-6
CONV_W = 4
LRU_WIDTH = 1024
LRU_HEADS = 8
LRU_HEAD_DIM = 128
LRU_C = 8.0
SSD_INNER = 2048
SSD_HEAD_DIM = 64
SSD_HEADS = 32
SSD_GROUPS = 4
SSD_STATE = 128
SSD_CHUNK = 128
SSD_PAIRS = SSD_HEADS // 2
N_EXPERTS = 32
TOP_K = 4
D_FF = 2048
SWIGLU_LIMIT = 7.0
SWIGLU_ALPHA = 1.702

LANES = 128
SUBLANES = 8
VMEM_LIMIT = 56 * 1024 * 1024

COL_XA, COL_YA, COL_Z, COL_XS, COL_BC, COL_GA, COL_GB = 0, 1024, 2048, 4096, 6144, 7168, 9216
PROJ_W = 11264


def _cparams(sem):
    return pltpu.CompilerParams(dimension_semantics=sem, vmem_limit_bytes=VMEM_LIMIT)


def _sigmoid(v):
    return 0.5 * jnp.tanh(0.5 * v) + 0.5


def _silu(v):
    return v * _sigmoid(v)


def _softplus(v):
    return jnp.maximum(v, 0.0) + jnp.log1p(jnp.exp(-jnp.abs(v)))


def _gelu_tanh(v):
    return 0.5 * v * (1.0 + jnp.tanh(0.7978845608028654 * (v + 0.044715 * v * v * v)))


def _pick(n, candidates):
    for c in candidates:
        if n % c == 0:
            return c
    raise ValueError(f"no tile for {n}")


def _inproj_kernel(x_ref, g_ref, w_ref, wdt_ref, o_ref, odt_ref, xn_ref):
    @pl.when(pl.program_id(1) == 0)
    def _():
        x = x_ref[...]
        ms = jnp.mean(x * x, axis=-1, keepdims=True)
        xn = (x * lax.rsqrt(ms + EPS) * g_ref[...]).astype(BF16)
        xn_ref[...] = xn
        odt_ref[...] = jnp.dot(xn, wdt_ref[...], preferred_element_type=F32)

    o_ref[...] = jnp.dot(xn_ref[...], w_ref[...], preferred_element_type=F32)


def _inproj(xr, g, w_main, w_dt):
    rows = xr.shape[0]
    tm = _pick(rows, (1024, 512, 256, 128))
    tn = 1024
    return pl.pallas_call(
        _inproj_kernel,
        grid=(rows // tm, PROJ_W // tn),
        in_specs=[
            pl.BlockSpec((tm, D_MODEL), lambda i, j: (i, 0)),
            pl.BlockSpec((1, D_MODEL), lambda i, j: (0, 0)),
            pl.BlockSpec((D_MODEL, tn), lambda i, j: (0, j)),
            pl.BlockSpec((D_MODEL, LANES), lambda i, j: (0, 0)),
        ],
        out_specs=[
            pl.BlockSpec((tm, tn), lambda i, j: (i, j)),
            pl.BlockSpec((tm, LANES), lambda i, j: (i, 0)),
        ],
        out_shape=[
            jax.ShapeDtypeStruct((rows, PROJ_W), F32),
            jax.ShapeDtypeStruct((rows, LANES), F32),
        ],
        scratch_shapes=[pltpu.VMEM((tm, D_MODEL), BF16)],
        compiler_params=_cparams(("parallel", "arbitrary")),
        name="inproj",
    )(xr, g, w_main, w_dt)


def _causal_conv(x, tail, cw, cb):
    n = x.shape[0]
    ext = jnp.concatenate([tail, x], axis=0)
    y = cb + cw[CONV_W - 1:CONV_W] * x
    for k in range(1, CONV_W):
        y = y + cw[CONV_W - 1 - k:CONV_W - k] * pltpu.roll(ext, k, axis=0)[SUBLANES:SUBLANES + n]
    return y


def _lru_kernel(xa_ref, ya_ref, cw_ref, cb_ref, wai_ref, ba_ref, bi_ref, lam_ref, tail0_ref, h0_ref,
                o_ref, tail_out_ref, h_out_ref, tail_sc, h_sc, *, tb, seq_start):
    t = pl.program_id(1)

    @pl.when(t == 0)
    def _():
        tail_sc[...] = tail0_ref[...]
        h_sc[...] = h0_ref[...]

    xa = xa_ref[...]
    xc = _causal_conv(xa, tail_sc[...], cw_ref[...], cb_ref[...])
    tail_sc[...] = xa[tb - SUBLANES:tb]

    xcb = xc.astype(BF16)
    r_parts, i_parts = [], []
    for h in range(LRU_HEADS):
        g = jnp.dot(xcb[:, h * LRU_HEAD_DIM:(h + 1) * LRU_HEAD_DIM], wai_ref[h],
                    preferred_element_type=F32)
        r_parts.append(g[:, :LRU_HEAD_DIM])
        i_parts.append(g[:, LRU_HEAD_DIM:])
    r = _sigmoid(jnp.concatenate(r_parts, axis=1) + ba_ref[...])
    ig = _sigmoid(jnp.concatenate(i_parts, axis=1) + bi_ref[...])

    log_a = LRU_C * r * (-_softplus(-lam_ref[...]))
    a = jnp.exp(log_a)
    mult = jnp.sqrt(-jnp.tanh(log_a) * (a * a + 1.0))
    row = lax.broadcasted_iota(jnp.int32, (tb, LRU_WIDTH), 0)
    if seq_start:
        mult = jnp.where((row == 0) & (t == 0), 1.0, mult)
    u = mult * (ig * xc)

    s = 1
    while s < tb:
        a_sh = pltpu.roll(a, s, axis=0)
        u_sh = pltpu.roll(u, s, axis=0)
        valid = row >= s
        u = jnp.where(valid, a * u_sh, 0.0) + u
        a = jnp.where(valid, a * a_sh, a)
        s *= 2
    h = a * h_sc[0:1, :] + u
    h_last = jnp.broadcast_to(h[tb - 1:tb, :], (SUBLANES, LRU_WIDTH))
    h_sc[...] = h_last
    o_ref[...] = (_gelu_tanh(ya_ref[...]) * h).astype(BF16)
    tail_out_ref[...] = xa[tb - SUBLANES:tb]
    h_out_ref[...] = h_last


def _lru(proj, nb, t_len, row0_blocks, tb, cw, cb, wai, ba, bi, lam, tail0, h0, seq_start):
    nt = t_len // tb
    kern = functools.partial(_lru_kernel, tb=tb, seq_start=seq_start)
    per_b = (proj.shape[0] // nb) // tb
    return pl.pallas_call(
        kern,
        grid=(nb, nt),
        in_specs=[
            pl.BlockSpec((tb, LRU_WIDTH), lambda b, t: (b * per_b + row0_blocks + t, COL_XA // LRU_WIDTH)),
            pl.BlockSpec((tb, LRU_WIDTH), lambda b, t: (b * per_b + row0_blocks + t, COL_YA // LRU_WIDTH)),
            pl.BlockSpec((CONV_W, LRU_WIDTH), lambda b, t: (0, 0)),
            pl.BlockSpec((1, LRU_WIDTH), lambda b, t: (0, 0)),
            pl.BlockSpec((LRU_HEADS, LRU_HEAD_DIM, 2 * LRU_HEAD_DIM), lambda b, t: (0, 0, 0)),
            pl.BlockSpec((1, LRU_WIDTH), lambda b, t: (0, 0)),
            pl.BlockSpec((1, LRU_WIDTH), lambda b, t: (0, 0)),
            pl.BlockSpec((1, LRU_WIDTH), lambda b, t: (0, 0)),
            pl.BlockSpec((SUBLANES, LRU_WIDTH), lambda b, t: (0, 0)),
            pl.BlockSpec((SUBLANES, LRU_WIDTH), lambda b, t: (0, 0)),
        ],
        out_specs=[
            pl.BlockSpec((tb, LRU_WIDTH), lambda b, t: (b * nt + t, 0)),
            pl.BlockSpec((None, SUBLANES, LRU_WIDTH), lambda b, t: (b, 0, 0)),
            pl.BlockSpec((None, SUBLANES, LRU_WIDTH), lambda b, t: (b, 0, 0)),
        ],
        out_shape=[
            jax.ShapeDtypeStruct((nb * t_len, LRU_WIDTH), BF16),
            jax.ShapeDtypeStruct((nb, SUBLANES, LRU_WIDTH), F32),
            jax.ShapeDtypeStruct((nb, SUBLANES, LRU_WIDTH), F32),
        ],
        scratch_shapes=[pltpu.VMEM((SUBLANES, LRU_WIDTH), F32), pltpu.VMEM((SUBLANES, LRU_WIDTH), F32)],
        compiler_params=_cparams(("arbitrary", "arbitrary")),
        name="lru",
    )(proj, proj, cw, cb, wai, ba, bi, lam, tail0, h0)


def _ssd_kernel(xs_ref, bc_ref, z_ref, dt_ref, cwx_ref, cbx_ref, cwb_ref, cbb_ref, dtb_ref, alog_ref,
                dexp_ref, ng_ref, tailx0_ref, tailb0_ref, s0_ref,
                y_ref, tailx_out_ref, tailb_out_ref, s_out_ref,
                tailx_sc, tailb_sc, s_sc, *, first_valid):
    c = pl.program_id(1)
    ln = SSD_CHUNK

    @pl.when(c == 0)
    def _():
        tailx_sc[...] = tailx0_ref[...]
        tailb_sc[...] = tailb0_ref[...]
        s_sc[...] = s0_ref[...]

    xr = xs_ref[...]
    br = bc_ref[...]
    xs = _silu(_causal_conv(xr, tailx_sc[...], cwx_ref[...], cbx_ref[...]))
    bc = _silu(_causal_conv(br, tailb_sc[...], cwb_ref[...], cbb_ref[...]))
    tailx_sc[...] = xr[ln - SUBLANES:ln]
    tailb_sc[...] = br[ln - SUBLANES:ln]
    dt = _softplus(dt_ref[...] + dtb_ref[...])
    if first_valid:
        keep = lax.broadcasted_iota(jnp.int32, (ln, 1), 0) >= first_valid
        xs = jnp.where(keep, xs, 0.0)
        bc = jnp.where(keep, bc, 0.0)
        dt = jnp.where(keep, dt, 0.0)

    a_neg = -jnp.exp(alog_ref[...])
    da = dt * a_neg
    rr = lax.broadcasted_iota(jnp.int32, (ln, ln), 0)
    cc = lax.broadcasted_iota(jnp.int32, (ln, ln), 1)
    causal = rr >= cc
    cs = jnp.dot(causal.astype(F32), da, preferred_element_type=F32,
                 precision=lax.Precision.HIGHEST)
    cs_last = cs[ln - 1:ln, :]
    ecs = jnp.exp(cs)
    wdec = jnp.exp(cs_last - cs) * dt
    cs_t = cs.T
    dt_t = dt.T
    wdec_t = wdec.T
    dec_col = jnp.exp(cs_t[:, ln - 1:ln])

    lane = lax.broadcasted_iota(jnp.int32, (1, LANES), 1)
    left = lane < SSD_HEAD_DIM
    xsb = xs.astype(BF16)
    y_parts = []
    for g in range(SSD_GROUPS):
        bg = bc[:, g * SSD_STATE:(g + 1) * SSD_STATE]
        cg = bc[:, (SSD_GROUPS + g) * SSD_STATE:(SSD_GROUPS + g + 1) * SSD_STATE]
        cb_mat = lax.dot_general(cg.astype(BF16), bg.astype(BF16), (((1,), (1,)), ((), ())),
                                 preferred_element_type=F32)
        bg_t = bg.T
        for jj in range(SSD_PAIRS // SSD_GROUPS):
            j = g * (SSD_PAIRS // SSD_GROUPS) + jj
            xp = xsb[:, j * LANES:(j + 1) * LANES]
            sp = s_sc[j]
            rhs = jnp.concatenate([xp, sp.astype(BF16)], axis=0)
            y_h, s_h, d_h = [], [], []
            for h in (2 * j, 2 * j + 1):
                seg = cs[:, h:h + 1] - cs_t[h:h + 1, :]
                lmat = jnp.exp(jnp.where(causal, seg, -jnp.inf)) * dt_t[h:h + 1, :]
                m_h = (cb_mat * lmat).astype(BF16)
                c_off = (cg * ecs[:, h:h + 1]).astype(BF16)
                y_h.append(jnp.dot(jnp.concatenate([m_h, c_off], axis=1), rhs,
                                   preferred_element_type=F32))
                bw_t = (bg_t * wdec_t[h:h + 1, :]).astype(BF16)
                s_h.append(jnp.dot(bw_t, xp, preferred_element_type=F32))
                d_h.append(dec_col[h:h + 1, :])
            y_parts.append(jnp.where(left, y_h[0], y_h[1]))
            s_sc[j] = sp * jnp.where(left, d_h[0], d_h[1]) + jnp.where(left, s_h[0], s_h[1])

    y = jnp.concatenate(y_parts, axis=1) + xs * dexp_ref[...]
    y = y * _silu(z_ref[...])
    gw = SSD_INNER // SSD_GROUPS
    outs = []
    for g in range(SSD_GROUPS):
        yg = y[:, g * gw:(g + 1) * gw]
        outs.append(yg * lax.rsqrt(jnp.mean(yg * yg, axis=-1, keepdims=True) + EPS))
    y_ref[...] = (jnp.concatenate(outs, axis=1) * ng_ref[...]).astype(BF16)
    tailx_out_ref[...] = xr[ln - SUBLANES:ln]
    tailb_out_ref[...] = br[ln - SUBLANES:ln]
    s_out_ref[...] = s_sc[...]


def _ssd(proj, dtraw, nb, t_len, row0_blocks, params, tailx0, tailb0, s0, first_valid):
    ln = SSD_CHUNK
    nc = t_len // ln
    per_b = (proj.shape[0] // nb) // ln
    cwx, cbx, cwb, cbb, dtb, alog, dexp, ng = params
    kern = functools.partial(_ssd_kernel, first_valid=first_valid)
    const2 = lambda b, c: (0, 0)
    rowi = lambda b, c: b * per_b + row0_blocks + c
    return pl.pallas_call(
        kern,
        grid=(nb, nc),
        in_specs=[
            pl.BlockSpec((ln, SSD_INNER), lambda b, c: (rowi(b, c), COL_XS // SSD_INNER)),
            pl.BlockSpec((ln, 1024), lambda b, c: (rowi(b, c), COL_BC // 1024)),
            pl.BlockSpec((ln, SSD_INNER), lambda b, c: (rowi(b, c), COL_Z // SSD_INNER)),
            pl.BlockSpec((ln, LANES), lambda b, c: (rowi(b, c), 0)),
            pl.BlockSpec((CONV_W, SSD_INNER), const2),
            pl.BlockSpec((1, SSD_INNER), const2),
            pl.BlockSpec((CONV_W, 1024), const2),
            pl.BlockSpec((1, 1024), const2),
            pl.BlockSpec((1, LANES), const2),
            pl.BlockSpec((1, LANES), const2),
            pl.BlockSpec((1, SSD_INNER), const2),
            pl.BlockSpec((1, SSD_INNER), const2),
            pl.BlockSpec((SUBLANES, SSD_INNER), const2),
            pl.BlockSpec((SUBLANES, 1024), const2),
            pl.BlockSpec((SSD_PAIRS, SSD_STATE, LANES), lambda b, c: (0, 0, 0)),
        ],
        out_specs=[
            pl.BlockSpec((ln, SSD_INNER), lambda b, c: (b * nc + c, 0)),
            pl.BlockSpec((None, SUBLANES, SSD_INNER), lambda b, c: (b, 0, 0)),
            pl.BlockSpec((None, SUBLANES, 1024), lambda b, c: (b, 0, 0)),
            pl.BlockSpec((None, SSD_PAIRS, SSD_STATE, LANES), lambda b, c: (b, 0, 0, 0)),
        ],
        out_shape=[
            jax.ShapeDtypeStruct((nb * t_len, SSD_INNER), BF16),
            jax.ShapeDtypeStruct((nb, SUBLANES, SSD_INNER), F32),
            jax.ShapeDtypeStruct((nb, SUBLANES, 1024), F32),
            jax.ShapeDtypeStruct((nb, SSD_PAIRS, SSD_STATE, LANES), F32),
        ],
        scratch_shapes=[
            pltpu.VMEM((SUBLANES, SSD_INNER), F32),
            pltpu.VMEM((SUBLANES, 1024), F32),
            pltpu.VMEM((SSD_PAIRS, SSD_STATE, LANES), F32),
        ],
        compiler_params=_cparams(("arbitrary", "arbitrary")),
        name="ssd",
    )(proj, proj, proj, dtraw, cwx, cbx, cwb, cbb, dtb, alog, dexp, ng, tailx0, tailb0, s0)


def _merge_kernel(a_ref, y_ref, ga_ref, gb_ref, wl_ref, ws_ref, gbias_ref, o_ref):
    oa = jnp.dot(a_ref[...], wl_ref[...], preferred_element_type=F32)
    ob = jnp.dot(y_ref[...], ws_ref[...], preferred_element_type=F32)
    gbias = gbias_ref[...]
    g_a = _sigmoid(ga_ref[...] + gbias[0:1, :])
    g_b = _sigmoid(gb_ref[...] + gbias[1:2, :])
    o_ref[...] = (g_a * oa + g_b * ob).astype(BF16)


def _merge(a_lru, y_ssd, proj, wl, ws, gbias):
    m = a_lru.shape[0]
    tm = _pick(m, (1024, 512, 256, 128))
    tn = 1024
    return pl.pallas_call(
        _merge_kernel,
        grid=(m // tm, D_MODEL // tn),
        in_specs=[
            pl.BlockSpec((tm, LRU_WIDTH), lambda i, j: (i, 0)),
            pl.BlockSpec((tm, SSD_INNER), lambda i, j: (i, 0)),
            pl.BlockSpec((tm, tn), lambda i, j: (i, COL_GA // tn + j)),
            pl.BlockSpec((tm, tn), lambda i, j: (i, COL_GB // tn + j)),
            pl.BlockSpec((LRU_WIDTH, tn), lambda i, j: (0, j)),
            pl.BlockSpec((SSD_INNER, tn), lambda i, j: (0, j)),
            pl.BlockSpec((2, tn), lambda i, j: (0, j)),
        ],
        out_specs=pl.BlockSpec((tm, tn), lambda i, j: (i, j)),
        out_shape=jax.ShapeDtypeStruct((m, D_MODEL), BF16),
        compiler_params=_cparams(("parallel", "arbitrary")),
        name="merge",
    )(a_lru, y_ssd, proj, proj, wl, ws, gbias)


def _outproj_kernel(m_ref, x_ref, wo_ref, g_ref, wr_ref, br_ref, h_ref, xn_ref, te_ref, tw_ref):
    h2 = x_ref[...] + jnp.dot(m_ref[...], wo_ref[...], preferred_element_type=F32)
    h_ref[...] = h2
    ms = jnp.mean(h2 * h2, axis=-1, keepdims=True)
    xn = h2 * lax.rsqrt(ms + EPS) * g_ref[...]
    xn_ref[...] = xn
    tm = xn.shape[0]
    x_hi = xn.astype(BF16)
    x_lo = (xn - x_hi.astype(F32)).astype(BF16)
    prod = jnp.dot(jnp.concatenate([x_hi, x_lo], axis=0), wr_ref[...], preferred_element_type=F32)
    top, bot = prod[:tm], prod[tm:]
    lane = lax.broadcasted_iota(jnp.int32, (tm, LANES), 1)
    logits = top + pltpu.roll(top, LANES - N_EXPERTS, axis=1) + bot + br_ref[...]
    vals = jnp.where(lane < N_EXPERTS, logits, -1e30)
    te = jnp.zeros((tm, LANES), jnp.int32)
    tw = jnp.zeros((tm, LANES), F32)
    m0 = None
    for k in range(TOP_K):
        mk = jnp.max(vals, axis=-1, keepdims=True)
        ik = jnp.min(jnp.where(vals == mk, lane, LANES), axis=-1, keepdims=True)
        if k == 0:
            m0 = mk
        te = jnp.where(lane == k, ik, te)
        tw = jnp.where(lane == k, jnp.exp(mk - m0), tw)
        vals = jnp.where(lane == ik, -jnp.inf, vals)
    tw = tw / jnp.sum(tw, axis=-1, keepdims=True)
    te_ref[...] = te
    tw_ref[...] = tw


def _outproj(merged, xrows, wo, g, wr, br):
    m = merged.shape[0]
    tm = _pick(m, (512, 256, 128))
    return pl.pallas_call(
        _outproj_kernel,
        grid=(m // tm,),
        in_specs=[
            pl.BlockSpec((tm, D_MODEL), lambda i: (i, 0)),
            pl.BlockSpec((tm, D_MODEL), lambda i: (i, 0)),
            pl.BlockSpec((D_MODEL, D_MODEL), lambda i: (0, 0)),
            pl.BlockSpec((1, D_MODEL), lambda i: (0, 0)),
            pl.BlockSpec((D_MODEL, LANES), lambda i: (0, 0)),
            pl.BlockSpec((1, LANES), lambda i: (0, 0)),
        ],
        out_specs=[
            pl.BlockSpec((tm, D_MODEL), lambda i: (i, 0)),
            pl.BlockSpec((tm, D_MODEL), lambda i: (i, 0)),
            pl.BlockSpec((tm, LANES), lambda i: (i, 0)),
            pl.BlockSpec((tm, LANES), lambda i: (i, 0)),
        ],
        out_shape=[
            jax.ShapeDtypeStruct((m, D_MODEL), F32),
            jax.ShapeDtypeStruct((m, D_MODEL), F32),
            jax.ShapeDtypeStruct((m, LANES), jnp.int32),
            jax.ShapeDtypeStruct((m, LANES), F32),
        ],
        compiler_params=_cparams(("parallel",)),
        name="outproj",
    )(merged, xrows, wo, g, wr, br)


def _moe_kernel(be_ref, nh_ref, nv_ref, idx_ref, idxn_ref, xn_hbm, wg_ref, wu_ref, wd_ref, bg_ref, bu_ref,
                bd_ref, o_ref, xbuf, xb16, sem, *, th, nf):
    b = pl.program_id(0)
    f = pl.program_id(1)
    nv = nv_ref[0]
    groups = 2 * th // SUBLANES

    def start_rows(idx_smem, g):
        for j in range(SUBLANES):
            tok = idx_smem[0, g * SUBLANES + j]
            pltpu.make_async_copy(
                xn_hbm.at[lax.shift_right_logical(tok, 3), pl.ds(lax.bitwise_and(tok, SUBLANES - 1), 1), :],
                xbuf.at[g, pl.ds(j, 1), :], sem.at[0]).start()

    def wait_gather():
        pltpu.make_async_copy(xn_hbm.at[pl.ds(0, groups)], xbuf, sem.at[0]).wait()

    @pl.when((f == 0) & (b < nv))
    def _():
        @pl.when(b == 0)
        def _():
            def body(g, carry):
                start_rows(idx_ref, g)
                return carry
            lax.fori_loop(0, groups, body, 0, unroll=8)

        wait_gather()
        xb16[...] = xbuf[...].reshape(2 * th, D_MODEL).astype(BF16)
        o_ref[...] = jnp.broadcast_to(bd_ref[...], o_ref.shape)

    def half(r0):
        x = xb16[r0:r0 + th, :]
        gate = jnp.dot(x, wg_ref[...].astype(BF16), preferred_element_type=F32) + bg_ref[...]
        up = jnp.dot(x, wu_ref[...].astype(BF16), preferred_element_type=F32) + bu_ref[...]
        gate = jnp.minimum(gate, SWIGLU_LIMIT)
        up = jnp.clip(up, -SWIGLU_LIMIT, SWIGLU_LIMIT)
        act = (up + 1.0) * gate * _sigmoid(SWIGLU_ALPHA * gate)
        o_ref[r0:r0 + th, :] += jnp.dot(act.astype(BF16), wd_ref[...].astype(BF16),
                                        preferred_element_type=F32)

    issue_steps = max(nf // 2, 1)

    @pl.when((b < nv) & (f < issue_steps))
    def _():
        for gi in range(groups // issue_steps):
            start_rows(idxn_ref, f * (groups // issue_steps) + gi)
        half(0)

    @pl.when((b < nv) & (f >= issue_steps))
    def _():
        half(0)

    @pl.when((b < nv) & (nh_ref[b] == 2))
    def _():
        half(th)

    @pl.when((f == nf - 1) & (b == nv - 1))
    def _():
        wait_gather()

    @pl.when((b >= nv) & (f == 0))
    def _():
        o_ref[...] = jnp.zeros(o_ref.shape, F32)


def _moe(block_e, nhalf, nvalid, row_tok3, xn, wgu, wd, bgu, bd, th, tf):
    nb = row_tok3.shape[0]
    tm = 2 * th
    nf = D_FF // tf
    kern = functools.partial(_moe_kernel, th=th, nf=nf)

    def bclamp(b, nv):
        return jnp.minimum(b, nv[0] - 1)

    def fclamp(b, f, nv):
        return jnp.where(b < nv[0], f, nf - 1)

    def wspec(shape, col):
        return pl.BlockSpec(shape, lambda b, f, be, nh, nv: (be[bclamp(b, nv)],) + col(fclamp(b, f, nv)))

    grid_spec = pltpu.PrefetchScalarGridSpec(
        num_scalar_prefetch=3,
        grid=(nb, nf),
        in_specs=[
            pl.BlockSpec((None, 1, tm), lambda b, f, be, nh, nv: (bclamp(b, nv), 0, 0),
                         memory_space=pltpu.SMEM),
            pl.BlockSpec((None, 1, tm), lambda b, f, be, nh, nv: (bclamp(b + 1, nv), 0, 0),
                         memory_space=pltpu.SMEM),
            pl.BlockSpec(memory_space=pl.ANY),
            wspec((None, D_MODEL, tf), lambda fc: (0, fc)),
            wspec((None, D_MODEL, tf), lambda fc: (0, nf + fc)),
            wspec((None, tf, D_MODEL), lambda fc: (fc, 0)),
            wspec((None, 1, tf), lambda fc: (0, fc)),
            wspec((None, 1, tf), lambda fc: (0, nf + fc)),
            wspec((None, 1, D_MODEL), lambda fc: (0, 0)),
        ],
        out_specs=pl.BlockSpec((tm, D_MODEL), lambda b, f, be, nh, nv: (b, 0)),
        scratch_shapes=[
            pltpu.VMEM((tm // SUBLANES, SUBLANES, D_MODEL), F32),
            pltpu.VMEM((tm, D_MODEL), BF16),
            pltpu.SemaphoreType.DMA((1,)),
        ],
    )
    xn_tiles = xn.reshape(xn.shape[0] // SUBLANES, SUBLANES, D_MODEL)
    return pl.pallas_call(
        kern,
        grid_spec=grid_spec,
        out_shape=jax.ShapeDtypeStruct((nb * tm, D_MODEL), F32),
        compiler_params=_cparams(("arbitrary", "arbitrary")),
        name="moe",
    )(block_e, nhalf, nvalid, row_tok3, row_tok3, xn_tiles, wgu, wgu, wd, bgu, bgu, bd)


def _combine_kernel(idx_ref, idxn_ref, y_hbm, h_ref, tw_ref, g_ref, o_ref, buf, sem, *, tc, nsteps):
    i = pl.program_id(0)
    nrow = TOP_K * tc

    def start_gather(idx_smem, slot):
        def body(r, carry):
            pltpu.make_async_copy(y_hbm.at[pl.ds(idx_smem[0, r], 1), :],
                                  buf.at[slot, pl.ds(r, 1), :], sem.at[slot]).start()
            return carry
        lax.fori_loop(0, nrow, body, 0, unroll=8)

    slot = lax.rem(i, 2)

    @pl.when(i == 0)
    def _():
        start_gather(idx_ref, 0)

    pltpu.make_async_copy(y_hbm.at[pl.ds(0, nrow), :], buf.at[slot], sem.at[slot]).wait()

    @pl.when(i + 1 < nsteps)
    def _():
        start_gather(idxn_ref, 1 - slot)

    tw = tw_ref[...]
    v = h_ref[...]
    for k in range(TOP_K):
        v = v + tw[:, k:k + 1] * buf[slot, pl.ds(k * tc, tc), :]
    ms = jnp.mean(v * v, axis=-1, keepdims=True)
    o_ref[...] = v * lax.rsqrt(ms + EPS) * g_ref[...]


def _combine(dest3, y, h2, tw, g, tc):
    m = h2.shape[0]
    nsteps = m // tc
    kern = functools.partial(_combine_kernel, tc=tc, nsteps=nsteps)
    return pl.pallas_call(
        kern,
        grid=(nsteps,),
        in_specs=[
            pl.BlockSpec((None, 1, TOP_K * tc), lambda i: (i, 0, 0), memory_space=pltpu.SMEM),
            pl.BlockSpec((None, 1, TOP_K * tc), lambda i: (jnp.minimum(i + 1, nsteps - 1), 0, 0),
                         memory_space=pltpu.SMEM),
            pl.BlockSpec(memory_space=pl.ANY),
            pl.BlockSpec((tc, D_MODEL), lambda i: (i, 0)),
            pl.BlockSpec((tc, LANES), lambda i: (i, 0)),
            pl.BlockSpec((1, D_MODEL), lambda i: (0, 0)),
        ],
        out_specs=pl.BlockSpec((tc, D_MODEL), lambda i: (i, 0)),
        out_shape=jax.ShapeDtypeStruct((m, D_MODEL), F32),
        scratch_shapes=[
            pltpu.VMEM((2, TOP_K * tc, D_MODEL), F32),
            pltpu.SemaphoreType.DMA((2,)),
        ],
        compiler_params=_cparams(("arbitrary",)),
        name="combine",
    )(dest3, dest3, y, h2, tw, g)


def _route(top_e, tm, th):
    m = top_e.shape[0]
    n_assign = m * TOP_K
    flat_e = top_e.reshape(-1)
    order = jnp.argsort(flat_e, stable=True).astype(jnp.int32)
    rank = jnp.argsort(order).astype(jnp.int32)
    counts = jnp.sum((flat_e[:, None] == jnp.arange(N_EXPERTS, dtype=jnp.int32)[None, :]).astype(jnp.int32),
                     axis=0)
    starts = jnp.cumsum(counts) - counts
    padded_counts = (counts + tm - 1) // tm * tm
    padded_end = jnp.cumsum(padded_counts)
    padded_start = padded_end - padded_counts
    shift = padded_start - starts
    dest = rank + shift[flat_e]
    n_blocks = n_assign // tm + N_EXPERTS
    block_row0 = jnp.arange(n_blocks, dtype=jnp.int32) * tm
    block_e = jnp.minimum(jnp.sum((block_row0[:, None] >= padded_end[None, :]).astype(jnp.int32), axis=1),
                          N_EXPERTS - 1).astype(jnp.int32)
    off = jnp.clip(block_row0 - shift[block_e], 0, n_assign)
    ctok = jnp.concatenate([order // TOP_K, jnp.zeros((tm,), jnp.int32)])
    row_tok = ctok[off[:, None] + jnp.arange(tm, dtype=jnp.int32)[None, :]]
    nvalid = (padded_end[-1] // tm).astype(jnp.int32).reshape(1)
    rows_in_block = jnp.clip(counts[block_e] - (block_row0 - padded_start[block_e]), 0, tm)
    nhalf = jnp.where(block_row0 < padded_end[-1], (rows_in_block + th - 1) // th, 0).astype(jnp.int32)
    return row_tok.reshape(n_blocks, 1, tm), dest.reshape(m, TOP_K), block_e, nhalf, nvalid


def kernel(x, meta_tokens, norm_mix_g, w_in, gate_bias, lru_conv_w, lru_conv_b, lru_wa, lru_ba, lru_wi,
           lru_bi, lru_lambda, ssd_conv_w, ssd_conv_b, ssd_dt_bias, ssd_A_log, ssd_D, ssd_norm_g,
           w_proj_lru, w_proj_ssd, w_out, norm_moe_g, w_router, b_router, w_gate_up, b_gate_up, w_down,
           b_down, norm_final_g):
    bsz, t_len, _ = x.shape
    m = bsz * t_len
    xrows = x.reshape(m, D_MODEL)

    w = w_in[0]
    dt_lo = COL_BC + 1024
    w_main = jnp.concatenate([w[:, :dt_lo], w[:, dt_lo + SSD_HEADS:]], axis=1).astype(BF16)
    w_dt = jnp.pad(w[:, dt_lo:dt_lo + SSD_HEADS], ((0, 0), (0, LANES - SSD_HEADS))).astype(BF16)
    g_mix = norm_mix_g[0].reshape(1, D_MODEL)
    wai = jnp.concatenate([lru_wa[0], lru_wi[0]], axis=-1).astype(BF16)
    lru_p = (lru_conv_w[0], lru_conv_b[0].reshape(1, -1), wai, lru_ba[0].reshape(1, -1),
             lru_bi[0].reshape(1, -1), lru_lambda[0].reshape(1, -1))
    scw, scb = ssd_conv_w[0], ssd_conv_b[0].reshape(1, -1)
    ssd_p = (scw[:, :SSD_INNER], scb[:, :SSD_INNER], scw[:, SSD_INNER:], scb[:, SSD_INNER:],
             jnp.pad(ssd_dt_bias[0], (0, LANES - SSD_HEADS)).reshape(1, LANES),
             jnp.pad(ssd_A_log[0], (0, LANES - SSD_HEADS), constant_values=-1e30).reshape(1, LANES),
             jnp.repeat(ssd_D[0], SSD_HEAD_DIM).reshape(1, SSD_INNER),
             ssd_norm_g[0].reshape(1, SSD_INNER))
    wl = w_proj_lru[0].astype(BF16)
    ws = w_proj_ssd[0].astype(BF16)
    wo = w_out[0].astype(BF16)
    wr_hi = w_router[0].astype(BF16)
    wr_lo = (w_router[0] - wr_hi.astype(F32)).astype(BF16)
    wr = jnp.pad(jnp.concatenate([wr_hi, wr_lo], axis=1), ((0, 0), (0, LANES - 2 * N_EXPERTS)))
    br = jnp.pad(b_router[0], (0, LANES - N_EXPERTS)).reshape(1, LANES)
    bgu =b_gate_up[0].reshape(N_EXPERTS, 1, 2 * D_FF)
    bd = b_down[0].reshape(N_EXPERTS, 1, D_MODEL)

    meta_rows = jnp.concatenate(
        [jnp.zeros((SSD_CHUNK - N_META, D_MODEL), F32), meta_tokens.astype(F32)], axis=0)
    proj_m, dt_m = _inproj(meta_rows, g_mix, w_main, w_dt)
    z8 = jnp.zeros((SUBLANES, LRU_WIDTH), F32)
    _, lru_tail, lru_h = _lru(proj_m, 1, N_META, (SSD_CHUNK - N_META) // N_META, N_META, *lru_p,
                              z8, z8, True)
    _, tailx, tailb, s_meta = _ssd(proj_m, dt_m, 1, SSD_CHUNK, 0, ssd_p,
                                   jnp.zeros((SUBLANES, SSD_INNER), F32),
                                   jnp.zeros((SUBLANES, 1024), F32),
                                   jnp.zeros((SSD_PAIRS, SSD_STATE, LANES), F32),
                                   SSD_CHUNK - N_META)

    proj, dtraw = _inproj(xrows, g_mix, w_main, w_dt)
    tb = _pick(t_len, (256, 128))
    a_lru, _, _ = _lru(proj, bsz, t_len, 0, tb, *lru_p, lru_tail[0], lru_h[0], False)
    y_ssd, _, _, _ = _ssd(proj, dtraw, bsz, t_len, 0, ssd_p, tailx[0], tailb[0], s_meta[0], 0)
    merged = _merge(a_lru, y_ssd, proj, wl, ws, gate_bias[0])
    h2, xn, te, tw = _outproj(merged, xrows, wo, norm_moe_g[0].reshape(1, D_MODEL), wr, br)

    th_e = 512
    tc = 128
    row_tok3, dest, block_e, nhalf, nvalid = _route(te[:, :TOP_K], 2 * th_e, th_e)
    y_rows = _moe(block_e, nhalf, nvalid, row_tok3, xn, w_gate_up[0], w_down[0], bgu, bd, th_e, 512)
    dest3 = dest.reshape(m // tc, tc, TOP_K).transpose(0, 2, 1).reshape(m // tc, 1, TOP_K * tc)
    out = _combine(dest3, y_rows, h2, tw, norm_final_g.reshape(1, D_MODEL), tc)
    return out.reshape(bsz, t_len, D_MODEL)
```
